```python
import jax, jax.numpy as jnp
from jax import lax
import numpy as np

D_MODEL = 1024
BATCH = 2
SEQ = 16384
DEPTH = 4

GRID_W = 64
N_META = 16
HEAD_DIM = 64
N_Q_HEADS = D_MODEL // HEAD_DIM
N_KV_HEADS = 4
GROUP = N_Q_HEADS // N_KV_HEADS
QKV_WIDTH = (N_Q_HEADS + 2 * N_KV_HEADS) * HEAD_DIM
Q_BLOCK = 128
ROPE_AXIS = HEAD_DIM // 2
ROPE_THETA = 10000.0
D_RNN = D_MODEL
N_RG_BLOCKS = 4
RG_BW = D_RNN // N_RG_BLOCKS
CONV_W = 4
CONV_LEFT = 2
RG_C = 8.0
D_FF = 2816
N_ATTN = (DEPTH + 1) // 2
N_REC = DEPTH // 2
NORM_EPS = 1e-6

kernel_name = "hybrid_gqa_rglru_macaron_encoder"


def rms_norm(x, g):
    xf = x.astype(jnp.float32)
    y = xf * lax.rsqrt(jnp.mean(xf * xf, axis=-1, keepdims=True) + NORM_EPS)
    return (y * g.astype(jnp.float32)).astype(x.dtype)


def swiglu(x, w_in, w_out):
    h = x @ w_in
    return (jax.nn.silu(h[..., :D_FF]) * h[..., D_FF:]) @ w_out


def axial_rope_tables(n_tok):
    rows = n_tok // GRID_W
    row = jnp.repeat(jnp.arange(rows, dtype=jnp.float32), GRID_W)
    col = jnp.tile(jnp.arange(GRID_W, dtype=jnp.float32), rows)
    zeros = jnp.zeros((N_META,), jnp.float32)
    row = jnp.concatenate([zeros, row])
    col = jnp.concatenate([zeros, col])
    inv = 1.0 / (ROPE_THETA ** (jnp.arange(0, ROPE_AXIS, 2, dtype=jnp.float32) / ROPE_AXIS))
    theta = jnp.concatenate([row[:, None] * inv, col[:, None] * inv], axis=-1)
    return jnp.cos(theta), jnp.sin(theta)


def apply_rope(x, cos, sin):
    shape = (1, cos.shape[0]) + (1,) * (x.ndim - 3) + (ROPE_AXIS,)
    c = cos.reshape(shape).astype(x.dtype)
    s = sin.reshape(shape).astype(x.dtype)
    x1, x2 = x[..., :ROPE_AXIS], x[..., ROPE_AXIS:]
    return jnp.concatenate([x1 * c - x2 * s, x1 * s + x2 * c], axis=-1)


def attention_mixer(x, w_qkv, q_gain, k_gain, w_o, cos, sin):
    B, L, _ = x.shape
    qkv = x @ w_qkv
    q = qkv[..., :N_Q_HEADS * HEAD_DIM].reshape(B, L, N_KV_HEADS, GROUP, HEAD_DIM)
    k = qkv[..., N_Q_HEADS * HEAD_DIM:(N_Q_HEADS + N_KV_HEADS) * HEAD_DIM].reshape(B, L, N_KV_HEADS, HEAD_DIM)
    v = qkv[..., (N_Q_HEADS + N_KV_HEADS) * HEAD_DIM:].reshape(B, L, N_KV_HEADS, HEAD_DIM)
    q = apply_rope(rms_norm(q, q_gain), cos, sin) * (HEAD_DIM ** -0.5)
    k = apply_rope(rms_norm(k, k_gain), cos, sin)

    def attend(qb):
        s = jnp.einsum('bqhgd,bkhd->bhgqk', qb, k).astype(jnp.float32)
        p = jax.nn.softmax(s, axis=-1).astype(v.dtype)
        return jnp.einsum('bhgqk,bkhd->bqhgd', p, v)

    o_meta = attend(q[:, :N_META])
    n_real = L - N_META
    n_blk = n_real // Q_BLOCK
    qr = q[:, N_META:].reshape(B, n_blk, Q_BLOCK, N_KV_HEADS, GROUP, HEAD_DIM).swapaxes(0, 1)
    o_real = lax.map(attend, qr).swapaxes(0, 1).reshape(B, n_real, N_KV_HEADS, GROUP, HEAD_DIM)
    o = jnp.concatenate([o_meta, o_real], axis=1).reshape(B, L, N_Q_HEADS * HEAD_DIM)
    return o @ w_o


def _lin_combine(left, right):
    a1, b1 = left
    a2, b2 = right
    return a1 * a2, a2 * b1 + b2


def recurrent_mixer(x, w_in, conv_w, conv_b, gate_w, gate_b, lam, w_out):
    B, L, _ = x.shape
    u = x @ w_in
    gate = jax.nn.gelu(u[..., :D_RNN])
    xr = u[..., D_RNN:]
    xp = jnp.pad(xr, ((0, 0), (CONV_LEFT, CONV_W - 1 - CONV_LEFT), (0, 0)))
    xc = conv_b + sum(xp[:, j:j + L] * conv_w[j] for j in range(CONV_W))
    xb = xc.reshape(B, L, N_RG_BLOCKS, RG_BW)
    g = jnp.einsum('blhi,dghij->dgblhj', xb, gate_w).reshape(2, 2, B, L, D_RNN)
    g = jax.nn.sigmoid((g + gate_b[:, :, None, None, :]).astype(jnp.float32))
    r, i = g[:, 0], g[:, 1]
    log_a = -RG_C * r * jax.nn.softplus(-lam.astype(jnp.float32))[:, None, None, :]
    a = jnp.exp(log_a)
    b = jnp.sqrt(-jnp.expm1(2.0 * log_a)) * i * xc.astype(jnp.float32)[None]
    _, h_f = lax.associative_scan(_lin_combine, (a[0], b[0]), axis=1)
    _, h_b = lax.associative_scan(_lin_combine, (a[1], b[1]), axis=1, reverse=True)
    y = (h_f + h_b).astype(x.dtype) * gate
    return y @ w_out


def setup_inputs(seed: int = 0) -> dict:
    key = jax.random.key(seed)
    ks = jax.random.split(key, 20)
    nrm = lambda k, shape, fan_in: jax.random.normal(k, shape, jnp.float32) * (fan_in ** -0.5)
    a0 = jax.random.uniform(ks[14], (N_REC, 2, D_RNN), jnp.float32, 0.9, 0.999)
    return {
        "x": jax.random.normal(ks[0], (BATCH, SEQ, D_MODEL), jnp.float32),
        "meta_tokens": jax.random.normal(ks[1], (N_META, D_MODEL), jnp.float32),
        "norm_gains": 1.0 + 0.05 * jax.random.normal(ks[2], (DEPTH, 6, D_MODEL), jnp.float32),
        "ffn_w_in": nrm(ks[3], (DEPTH, 2, D_MODEL, 2 * D_FF), D_MODEL),
        "ffn_w_out": nrm(ks[4], (DEPTH, 2, D_FF, D_MODEL), D_FF),
        "attn_w_qkv": nrm(ks[5], (N_ATTN, D_MODEL, QKV_WIDTH), D_MODEL),
        "attn_q_gain": 1.0 + 0.05 * jax.random.normal(ks[6], (N_ATTN, HEAD_DIM), jnp.float32),
        "attn_k_gain": 1.0 + 0.05 * jax.random.normal(ks[7], (N_ATTN, HEAD_DIM), jnp.float32),
        "attn_w_o": nrm(ks[8], (N_ATTN, N_Q_HEADS * HEAD_DIM, D_MODEL), N_Q_HEADS * HEAD_DIM),
        "rec_w_in": nrm(ks[9], (N_REC, D_MODEL, 2 * D_RNN), D_MODEL),
        "rec_conv_w": nrm(ks[10], (N_REC, CONV_W, D_RNN), CONV_W),
        "rec_conv_b": 0.01 * jax.random.normal(ks[11], (N_REC, D_RNN), jnp.float32),
        "rec_gate_w": nrm(ks[12], (N_REC, 2, 2, N_RG_BLOCKS, RG_BW, RG_BW), RG_BW),
        "rec_gate_b": 0.01 * jax.random.normal(ks[13], (N_REC, 2, 2, D_RNN), jnp.float32),
        "rec_lambda": jnp.log(a0) - jnp.log1p(-a0),
        "rec_w_out": nrm(ks[15], (N_REC, D_RNN, D_MODEL), D_RNN),
    }


def reference(x, meta_tokens, norm_gains, ffn_w_in, ffn_w_out, attn_w_qkv, attn_q_gain,
              attn_k_gain, attn_w_o, rec_w_in, rec_conv_w, rec_conv_b, rec_gate_w,
              rec_gate_b, rec_lambda, rec_w_out):
    B, n_tok, _ = x.shape
    h = jnp.concatenate([jnp.broadcast_to(meta_tokens.astype(x.dtype)[None], (B, N_META, D_MODEL)), x], axis=1)
    cos, sin = axial_rope_tables(n_tok)
    for layer in range(DEPTH):
        g = norm_gains[layer]
        h = h + 0.5 * rms_norm(swiglu(rms_norm(h, g[0]), ffn_w_in[layer, 0], ffn_w_out[layer, 0]), g[1])
        hn = rms_norm(h, g[2])
        if layer % 2 == 0:
            j = layer // 2
            m = attention_mixer(hn, attn_w_qkv[j], attn_q_gain[j], attn_k_gain[j], attn_w_o[j], cos, sin)
        else:
            j = layer // 2
            m = recurrent_mixer(hn, rec_w_in[j], rec_conv_w[j], rec_conv_b[j], rec_gate_w[j],
                                rec_gate_b[j], rec_lambda[j], rec_w_out[j])
        h = h + rms_norm(m, g[3])
        h = h + 0.5 * rms_norm(swiglu(rms_norm(h, g[4]), ffn_w_in[layer, 1], ffn_w_out[layer, 1]), g[5])
    return h[:, N_META:]
```

```python
import functools

import jax
import jax.numpy as jnp
from jax import lax
from jax.experimental import pallas as pl
from jax.experimental.pallas import tpu as pltpu

D_MODEL = 1024
N_META = 16
GRID_W = 64
HEAD_DIM = 64
N_Q_HEADS = 16
N_KV_HEADS = 4
GROUP = 4
QKV_WIDTH = 1536
ROPE_AXIS = 32
ROPE_THETA = 10000.0
D_RNN = 1024
N_RG_BLOCKS = 4
RG_BW = 256
RG_C = 8.0
D_FF = 2816
NORM_EPS = 1e-6

VMEM_LIMIT_BYTES = 56 * 1024 * 1024

TILE = 256
FFN_TM = 512
V_ROWS = 80
NEG_BIG = -1e30

BF16 = jnp.bfloat16
F32 = jnp.float32


def _cparams(sem):
    return pltpu.CompilerParams(dimension_semantics=sem, vmem_limit_bytes=VMEM_LIMIT_BYTES)


def _rms_rows(x, g):
    return x * lax.rsqrt(jnp.mean(x * x, axis=-1, keepdims=True) + NORM_EPS) * g


def _const_spec(shape):
    zeros = (0,) * len(shape)
    return pl.BlockSpec(shape, lambda *_: zeros)


FF_CHUNKS = (768, 768, 768, 512)


def _ffn_body(h_ref, gpre_ref, gpost_ref, win_ref, wout_ref, o_ref):
    x = h_ref[...]
    xn = _rms_rows(x, gpre_ref[...]).astype(BF16)
    acc = jnp.zeros(x.shape, F32)
    c0 = 0
    for ck in FF_CHUNKS:
        a = jnp.dot(xn, win_ref[:, c0:c0 + ck], preferred_element_type=F32)
        b = jnp.dot(xn, win_ref[:, D_FF + c0:D_FF + c0 + ck], preferred_element_type=F32)
        g = (a * jax.nn.sigmoid(a) * b).astype(BF16)
        acc = acc + jnp.dot(g, wout_ref[c0:c0 + ck, :], preferred_element_type=F32)
        c0 += ck
    o_ref[...] = x + 0.5 * _rms_rows(acc, gpost_ref[...])


def _ffn(h, gpre, gpost, w_in, w_out):
    tp = h.shape[0]
    return pl.pallas_call(
        _ffn_body,
        grid=(tp // FFN_TM,),
        in_specs=[
            pl.BlockSpec((FFN_TM, D_MODEL), lambda i: (i, 0)),
            _const_spec((1, D_MODEL)),
            _const_spec((1, D_MODEL)),
            pl.BlockSpec((D_MODEL, 2 * D_FF), lambda i: (0, 0), pipeline_mode=pl.Buffered(1)),
            pl.BlockSpec((D_FF, D_MODEL), lambda i: (0, 0), pipeline_mode=pl.Buffered(1)),
        ],
        out_specs=pl.BlockSpec((FFN_TM, D_MODEL), lambda i: (i, 0)),
        out_shape=jax.ShapeDtypeStruct(h.shape, F32),
        compiler_params=_cparams(("arbitrary",)),
        name="ffn",
    )(h, gpre.reshape(1, -1), gpost.reshape(1, -1), w_in, w_out)


def _head_norm_rope(blk, gain, cos, sin):
    y = blk * lax.rsqrt(jnp.mean(blk * blk, axis=0, keepdims=True) + NORM_EPS) * gain
    x1 = y[:ROPE_AXIS]
    x2 = y[ROPE_AXIS:]
    return jnp.concatenate([x1 * cos - x2 * sin, x1 * sin + x2 * cos], axis=0)


def _qkv_body(h_ref, g_ref, wt_ref, qg_ref, kg_ref, cos_ref, sin_ref, qt_ref, k_ref, vt_ref):
    hn = _rms_rows(h_ref[...], g_ref[...]).astype(BF16)
    qkv_t = lax.dot_general(wt_ref[...], hn, (((1,), (1,)), ((), ())),
                            preferred_element_type=F32)
    cos = cos_ref[...]
    sin = sin_ref[...]
    qg = qg_ref[...]
    kg = kg_ref[...]
    for hq in range(N_Q_HEADS):
        blk = qkv_t[hq * HEAD_DIM:(hq + 1) * HEAD_DIM]
        q = _head_norm_rope(blk, qg, cos, sin) * (HEAD_DIM ** -0.5)
        qt_ref[hq * HEAD_DIM:(hq + 1) * HEAD_DIM, :] = q.astype(BF16)
    k_off = N_Q_HEADS * HEAD_DIM
    k_heads = []
    for hk in range(N_KV_HEADS):
        blk = qkv_t[k_off + hk * HEAD_DIM:k_off + (hk + 1) * HEAD_DIM]
        k_heads.append(_head_norm_rope(blk, kg, cos, sin))
    k_t = jnp.concatenate(k_heads, axis=0)
    k_ref[0] = k_t.T.astype(BF16)
    v_off = k_off + N_KV_HEADS * HEAD_DIM
    tm = hn.shape[0]
    ones_rows = (lax.broadcasted_iota(jnp.int32, (V_ROWS - HEAD_DIM, tm), 0) == 0).astype(BF16)
    for hk in range(N_KV_HEADS):
        v = qkv_t[v_off + hk * HEAD_DIM:v_off + (hk + 1) * HEAD_DIM].astype(BF16)
        vt_ref[hk, 0] = jnp.concatenate([v, ones_rows], axis=0)


def _attn_qkv(h, g, w_t, q_gain, k_gain, cos_t, sin_t):
    tp = h.shape[0]
    nt = tp // TILE
    return pl.pallas_call(
        _qkv_body,
        grid=(nt,),
        in_specs=[
            pl.BlockSpec((TILE, D_MODEL), lambda i: (i, 0)),
            _const_spec((1, D_MODEL)),
            _const_spec((QKV_WIDTH, D_MODEL)),
            _const_spec((HEAD_DIM, 1)),
            _const_spec((HEAD_DIM, 1)),
            pl.BlockSpec((ROPE_AXIS, TILE), lambda i: (0, i)),
            pl.BlockSpec((ROPE_AXIS, TILE), lambda i: (0, i)),
        ],
        out_specs=[
            pl.BlockSpec((N_Q_HEADS * HEAD_DIM, TILE), lambda i: (0, i)),
            pl.BlockSpec((1, TILE, N_KV_HEADS * HEAD_DIM), lambda i: (i, 0, 0)),
            pl.BlockSpec((N_KV_HEADS, 1, V_ROWS, TILE), lambda i: (0, i, 0, 0)),
        ],
        out_shape=[
            jax.ShapeDtypeStruct((N_Q_HEADS * HEAD_DIM, tp), BF16),
            jax.ShapeDtypeStruct((nt, TILE, N_KV_HEADS * HEAD_DIM), BF16),
            jax.ShapeDtypeStruct((N_KV_HEADS, nt, V_ROWS, TILE), BF16),
        ],
        compiler_params=_cparams(("arbitrary",)),
        name="attn_qkv",
    )(h, g.reshape(1, -1), w_t, q_gain.reshape(-1, 1), k_gain.reshape(-1, 1), cos_t, sin_t)


def _attn_body(qt_ref, k_ref, vt_ref, o_ref, qaug_ref, *, n_chunks, n_valid_last):
    kvh = pl.program_id(1)
    tq = TILE
    qaug_ref[...] = jnp.zeros(qaug_ref.shape, BF16)
    row0 = pl.multiple_of(kvh * HEAD_DIM, HEAD_DIM)
    for g in range(GROUP):
        qaug_ref[pl.ds(row0, HEAD_DIM), g * tq:(g + 1) * tq] = qt_ref[g * HEAD_DIM:(g + 1) * HEAD_DIM, :]
    qa = qaug_ref[...]

    def chunk(c, carry, masked):
        m, acc = carry
        s = jnp.dot(k_ref[c], qa, preferred_element_type=F32)
        if masked:
            rows = lax.broadcasted_iota(jnp.int32, s.shape, 0)
            s = jnp.where(rows < n_valid_last, s, NEG_BIG)
        m_new = jnp.maximum(m, jnp.max(s, axis=0, keepdims=True))
        alpha = jnp.exp(m - m_new)
        p = jnp.exp(s - m_new).astype(BF16)
        acc = alpha * acc + jnp.dot(vt_ref[0, c], p, preferred_element_type=F32)
        return m_new, acc

    init = (jnp.full((1, GROUP * tq), NEG_BIG, F32), jnp.zeros((V_ROWS, GROUP * tq), F32))
    carry = lax.fori_loop(0, n_chunks - 1, lambda c, cr: chunk(c, cr, False), init)
    _, acc = chunk(n_chunks - 1, carry, True)
    out_t = acc[:HEAD_DIM] / acc[HEAD_DIM:HEAD_DIM + 1]
    for g in range(GROUP):
        o_ref[:, g * HEAD_DIM:(g + 1) * HEAD_DIM] = out_t[:, g * tq:(g + 1) * tq].T.astype(BF16)


def _attention(q_t, k, v_t, batch, n_valid_last):
    tp = q_t.shape[1]
    n_chunks = tp // TILE // batch
    body = functools.partial(_attn_body, n_chunks=n_chunks, n_valid_last=n_valid_last)
    return pl.pallas_call(
        body,
        grid=(batch, N_KV_HEADS, n_chunks),
        in_specs=[
            pl.BlockSpec((GROUP * HEAD_DIM, TILE), lambda b, h, i: (h, b * n_chunks + i)),
            pl.BlockSpec((n_chunks, TILE, N_KV_HEADS * HEAD_DIM), lambda b, h, i: (b, 0, 0)),
            pl.BlockSpec((1, n_chunks, V_ROWS, TILE), lambda b, h, i: (h, b, 0, 0)),
        ],
        out_specs=pl.BlockSpec((TILE, GROUP * HEAD_DIM), lambda b, h, i: (b * n_chunks + i, h)),
        out_shape=jax.ShapeDtypeStruct((tp, N_Q_HEADS * HEAD_DIM), BF16),
        scratch_shapes=[pltpu.VMEM((N_KV_HEADS * HEAD_DIM, GROUP * TILE), BF16)],
        compiler_params=_cparams(("arbitrary", "arbitrary", "arbitrary")),
        name="attention",
    )(q_t, k, v_t)


def _proj_res_body(a_ref, h_ref, w_ref, g_ref, o_ref):
    m = jnp.dot(a_ref[...], w_ref[...], preferred_element_type=F32)
    o_ref[...] = h_ref[...] + _rms_rows(m, g_ref[...])


def _proj_residual(a, h, w, g):
    tp, kdim = a.shape
    return pl.pallas_call(
        _proj_res_body,
        grid=(tp // FFN_TM,),
        in_specs=[
            pl.BlockSpec((FFN_TM, kdim), lambda i: (i, 0)),
            pl.BlockSpec((FFN_TM, D_MODEL), lambda i: (i, 0)),
            _const_spec((kdim, D_MODEL)),
            _const_spec((1, D_MODEL)),
        ],
        out_specs=pl.BlockSpec((FFN_TM, D_MODEL), lambda i: (i, 0)),
        out_shape=jax.ShapeDtypeStruct(h.shape, F32),
        compiler_params=_cparams(("arbitrary",)),
        name="proj_residual",
    )(a, h, w, g.reshape(1, -1))


def _rec_in_body(h_ref, g_ref, w_ref, gate_ref, xr_ref):
    hn = _rms_rows(h_ref[...], g_ref[...]).astype(BF16)
    u = jnp.dot(hn, w_ref[...], preferred_element_type=F32)
    x = u[:, :D_RNN]
    cdf = 0.5 * (1.0 + jnp.tanh(0.7978845608028654 * (x + 0.044715 * (x * x * x))))
    gate_ref[...] = x * cdf
    xr_ref[...] = u[:, D_RNN:]


def _rec_in(h, g, w):
    tp = h.shape[0]
    spec = pl.BlockSpec((FFN_TM, D_MODEL), lambda i: (i, 0))
    return pl.pallas_call(
        _rec_in_body,
        grid=(tp // FFN_TM,),
        in_specs=[spec, _const_spec((1, D_MODEL)), _const_spec((D_MODEL, 2 * D_RNN))],
        out_specs=[spec, spec],
        out_shape=[jax.ShapeDtypeStruct((tp, D_RNN), F32)] * 2,
        compiler_params=_cparams(("arbitrary",)),
        name="rec_in",
    )(h, g.reshape(1, -1), w)


def _conv_gates(x, prev8, next8, has_prev, has_next, last_row, n_valid_rows,
                cw_ref, cb_ref, wr_ref, wi_ref, br_ref, bi_ref, lam_ref):
    tt = x.shape[0]
    rows = lax.broadcasted_iota(jnp.int32, x.shape, 0)
    p7 = jnp.where(has_prev, prev8[7:8], 0.0)
    p6 = jnp.where(has_prev, prev8[6:7], 0.0)
    n0 = jnp.where(has_next, next8[0:1], 0.0)
    xm1 = jnp.where(rows == 0, p7, pltpu.roll(x, 1, axis=0))
    xm2 = jnp.where(rows == 0, p6, jnp.where(rows == 1, p7, pltpu.roll(x, 2, axis=0)))
    xp1 = jnp.where(rows == last_row, n0, pltpu.roll(x, tt - 1, axis=0))
    cw = cw_ref[...]
    xc = cb_ref[...] + xm2 * cw[0:1] + xm1 * cw[1:2] + x * cw[2:3] + xp1 * cw[3:4]
    softplus = jnp.logaddexp(-lam_ref[...], 0.0)
    r_parts = []
    i_parts = []
    for blk in range(N_RG_BLOCKS):
        xb = xc[:, blk * RG_BW:(blk + 1) * RG_BW].astype(BF16)
        r_parts.append(jnp.dot(xb, wr_ref[blk], preferred_element_type=F32))
        i_parts.append(jnp.dot(xb, wi_ref[blk], preferred_element_type=F32))
    r = jax.nn.sigmoid(jnp.concatenate(r_parts, axis=1) + br_ref[...])
    i = jax.nn.sigmoid(jnp.concatenate(i_parts, axis=1) + bi_ref[...])
    log_a = -RG_C * r * softplus
    a = jnp.exp(log_a)
    b = jnp.sqrt(1.0 - a * a) * i * xc
    valid = rows < n_valid_rows
    return jnp.where(valid, a, 1.0), jnp.where(valid, b, 0.0)


def _scan_fwd_body(x_ref, prev_ref, next_ref, cw_ref, cb_ref, wr_ref, wi_ref, br_ref, bi_ref,
                   lam_ref, hf_ref, a_s, b_s, carry_s, *, n_tiles):
    j = pl.program_id(1)
    is_tail = j == 0
    a, b = _conv_gates(
        x_ref[...], prev_ref[...], next_ref[...],
        has_prev=j > 0, has_next=j < n_tiles - 1,
        last_row=jnp.where(is_tail, N_META - 1, TILE - 1),
        n_valid_rows=jnp.where(is_tail, N_META, TILE),
        cw_ref=cw_ref, cb_ref=cb_ref, wr_ref=wr_ref, wi_ref=wi_ref,
        br_ref=br_ref, bi_ref=bi_ref, lam_ref=lam_ref)
    a_s[...] = a
    b_s[...] = b

    @pl.when(j == 0)
    def _():
        carry_s[...] = jnp.zeros(carry_s.shape, F32)

    def step(r, c):
        c = a_s[pl.ds(r, 1), :] * c + b_s[pl.ds(r, 1), :]
        hf_ref[pl.ds(r, 1), :] = c
        return c

    carry_s[...] = lax.fori_loop(0, TILE, step, carry_s[...], unroll=8)


def _scan_bwd_body(x_ref, prev_ref, next_ref, cw_ref, cb_ref, wr_ref, wi_ref, br_ref, bi_ref,
                   lam_ref, hf_ref, gate_ref, h_ref, wout_ref, g_ref, o_ref,
                   a_s, b_s, hb_s, carry_s, *, n_tiles):
    j = pl.program_id(1)
    is_tail = j == n_tiles - 1
    a, b = _conv_gates(
        x_ref[...], prev_ref[...], next_ref[...],
        has_prev=j < n_tiles - 1, has_next=j > 0,
        last_row=jnp.where(is_tail, N_META - 1, TILE - 1),
        n_valid_rows=jnp.where(is_tail, N_META, TILE),
        cw_ref=cw_ref, cb_ref=cb_ref, wr_ref=wr_ref, wi_ref=wi_ref,
        br_ref=br_ref, bi_ref=bi_ref, lam_ref=lam_ref)
    a_s[...] = a
    b_s[...] = b

    @pl.when(j == 0)
    def _():
        carry_s[...] = jnp.zeros(carry_s.shape, F32)

    def step(t, c):
        r = TILE - 1 - t
        c = a_s[pl.ds(r, 1), :] * c + b_s[pl.ds(r, 1), :]
        hb_s[pl.ds(r, 1), :] = c
        return c

    carry_s[...] = lax.fori_loop(0, TILE, step, carry_s[...], unroll=8)
    y = ((hf_ref[...] + hb_s[...]) * gate_ref[...]).astype(BF16)
    m = jnp.dot(y, wout_ref[...], preferred_element_type=F32)
    o_ref[...] = h_ref[...] + _rms_rows(m, g_ref[...])


def _scan_specs(batch, n_tiles, reverse):
    n_x = n_tiles - 1
    rpt = TILE // 8

    def logical(j):
        return (n_tiles - 1 - j) if reverse else j

    def phys_tile(b, j):
        lj = logical(j)
        return b * n_tiles + jnp.where(lj == 0, n_x, lj - 1)

    def prev_blk(b, j):
        lj = logical(j)
        base = b * n_tiles * rpt
        idx = jnp.where(lj <= 1, n_x * rpt + 1, (lj - 1) * rpt - 1)
        return base + idx

    def next_blk(b, j):
        lj = logical(j)
        base = b * n_tiles * rpt
        idx = jnp.where(lj == 0, 0, jnp.minimum(lj, n_x - 1) * rpt)
        return base + idx

    tile_spec = pl.BlockSpec((TILE, D_RNN), lambda b, j: (phys_tile(b, j), 0))
    prev_spec = pl.BlockSpec((8, D_RNN), lambda b, j: (prev_blk(b, j), 0))
    next_spec = pl.BlockSpec((8, D_RNN), lambda b, j: (next_blk(b, j), 0))
    return tile_spec, prev_spec, next_spec


def _gate_specs():
    return [
        _const_spec((4, D_RNN)),
        _const_spec((1, D_RNN)),
        _const_spec((N_RG_BLOCKS, RG_BW, RG_BW)),
        _const_spec((N_RG_BLOCKS, RG_BW, RG_BW)),
        _const_spec((1, D_RNN)),
        _const_spec((1, D_RNN)),
        _const_spec((1, D_RNN)),
    ]


def _scan_fwd(xr, batch, conv_w, conv_b, wr, wi, br, bi, lam):
    tp = xr.shape[0]
    n_tiles = tp // TILE // batch
    tile_spec, prev_spec, next_spec = _scan_specs(batch, n_tiles, reverse=False)
    return pl.pallas_call(
        functools.partial(_scan_fwd_body, n_tiles=n_tiles),
        grid=(batch, n_tiles),
        in_specs=[tile_spec, prev_spec, next_spec] + _gate_specs(),
        out_specs=tile_spec,
        out_shape=jax.ShapeDtypeStruct((tp, D_RNN), F32),
        scratch_shapes=[pltpu.VMEM((TILE, D_RNN), F32), pltpu.VMEM((TILE, D_RNN), F32),
                        pltpu.VMEM((1, D_RNN), F32)],
        compiler_params=_cparams(("arbitrary", "arbitrary")),
        name="scan_fwd",
    )(xr, xr, xr, conv_w, conv_b, wr, wi, br, bi, lam)


def _scan_bwd(xr, batch, conv_w, conv_b, wr, wi, br, bi, lam, hf, gate, h, w_out, g):
    tp = xr.shape[0]
    n_tiles = tp // TILE // batch
    tile_spec, prev_spec, next_spec = _scan_specs(batch, n_tiles, reverse=True)
    return pl.pallas_call(
        functools.partial(_scan_bwd_body, n_tiles=n_tiles),
        grid=(batch, n_tiles),
        in_specs=[tile_spec, prev_spec, next_spec] + _gate_specs()
        + [tile_spec, tile_spec, tile_spec, _const_spec((D_RNN, D_MODEL)), _const_spec((1, D_MODEL))],
        out_specs=tile_spec,
        out_shape=jax.ShapeDtypeStruct((tp, D_MODEL), F32),
        scratch_shapes=[pltpu.VMEM((TILE, D_RNN), F32), pltpu.VMEM((TILE, D_RNN), F32),
                        pltpu.VMEM((TILE, D_RNN), F32), pltpu.VMEM((1, D_RNN), F32)],
        compiler_params=_cparams(("arbitrary", "arbitrary")),
        name="scan_bwd",
    )(xr, xr, xr, conv_w, conv_b, wr, wi, br, bi, lam, hf, gate, h, w_out, g.reshape(1, -1))


def _rope_tables_t(n_tok, lp, batch):
    n = jnp.arange(n_tok, dtype=jnp.int32)
    row = (n // GRID_W).astype(F32)
    col = (n % GRID_W).astype(F32)
    zeros = jnp.zeros((lp - n_tok,), F32)
    row = jnp.concatenate([row, zeros])
    col = jnp.concatenate([col, zeros])
    inv = 1.0 / (ROPE_THETA ** (jnp.arange(0, ROPE_AXIS, 2, dtype=F32) / ROPE_AXIS))
    theta_t = jnp.concatenate([inv[:, None] * row[None, :], inv[:, None] * col[None, :]], axis=0)
    theta_t = jnp.tile(theta_t, (1, batch))
    return jnp.cos(theta_t), jnp.sin(theta_t)


def kernel(x, meta_tokens, norm_gains, ffn_w_in, ffn_w_out, attn_w_qkv, attn_q_gain, attn_k_gain,
           attn_w_o, rec_w_in, rec_conv_w, rec_conv_b, rec_gate_w, rec_gate_b, rec_lambda, rec_w_out):
    batch, n_tok, d = x.shape
    assert d == D_MODEL and n_tok % TILE == 0 and N_META <= TILE
    lp = n_tok + TILE
    assert (batch * lp) % FFN_TM == 0
    depth = norm_gains.shape[0]

    meta = jnp.broadcast_to(meta_tokens.astype(x.dtype)[None], (batch, N_META, d))
    pad = jnp.zeros((batch, TILE - N_META, d), x.dtype)
    h = jnp.concatenate([x, meta, pad], axis=1).reshape(batch * lp, d)
    cos_t, sin_t = _rope_tables_t(n_tok, lp, batch)

    for layer in range(depth):
        g = norm_gains[layer]
        h = _ffn(h, g[0], g[1], ffn_w_in[layer, 0].astype(BF16), ffn_w_out[layer, 0].astype(BF16))
        j = layer // 2
        if layer % 2 == 0:
            q_t, k, v_t = _attn_qkv(h, g[2], attn_w_qkv[j].T.astype(BF16), attn_q_gain[j],
                                    attn_k_gain[j], cos_t, sin_t)
            o = _attention(q_t, k, v_t, batch, N_META)
            h = _proj_residual(o, h, attn_w_o[j].astype(BF16), g[3])
        else:
            gate, xr = _rec_in(h, g[2], rec_w_in[j].astype(BF16))
            gw = rec_gate_w[j].astype(BF16)
            gb = rec_gate_b[j]
            lam = rec_lambda[j]
            cw = rec_conv_w[j]
            cb = rec_conv_b[j].reshape(1, -1)
            hf = _scan_fwd(xr, batch, cw, cb, gw[0, 0], gw[0, 1], gb[0, 0].reshape(1, -1),
                           gb[0, 1].reshape(1, -1), lam[0].reshape(1, -1))
            h = _scan_bwd(xr, batch, cw, cb, gw[1, 0], gw[1, 1], gb[1, 0].reshape(1, -1),
                          gb[1, 1].reshape(1, -1), lam[1].reshape(1, -1), hf, gate, h,
                          rec_w_out[j].astype(BF16), g[3])
        h = _ffn(h, g[4], g[5], ffn_w_in[layer, 1].astype(BF16), ffn_w_out[layer, 1].astype(BF16))
    return h.reshape(batch, lp, d)[:, :n_tok]
```

```python
import functools

import jax
import jax.numpy as jnp
from jax import lax
from jax.experimental import pallas as pl
from jax.experimental.pallas import tpu as pltpu

D_MODEL = 1024
N_META = 16
GRID_W = 64
HEAD_DIM = 64
N_Q_HEADS = 16
N_KV_HEADS = 4
GROUP = 4
QKV_WIDTH = 1536
ROPE_AXIS = 32
ROPE_THETA = 10000.0
D_RNN = 1024
N_RG_BLOCKS = 4
RG_BW = 256
RG_C = 8.0
D_FF = 2816
NORM_EPS = 1e-6

VMEM_LIMIT_BYTES = 56 * 1024 * 1024

TILE = 256
ATT_TK = 1280
LOG2E = 1.4426950408889634
FFN_TM = 512
V_ROWS = 80
NEG_BIG = -1e30

BF16 = jnp.bfloat16
F32 = jnp.float32


def _cparams(sem):
    return pltpu.CompilerParams(dimension_semantics=sem, vmem_limit_bytes=VMEM_LIMIT_BYTES)


def _rms_rows(x, g):
    return x * lax.rsqrt(jnp.mean(x * x, axis=-1, keepdims=True) + NORM_EPS) * g


def _const_spec(shape):
    zeros = (0,) * len(shape)
    return pl.BlockSpec(shape, lambda *_: zeros)


FF_CHUNKS = (768, 768, 768, 512)


def _ffn_body(h_ref, gpre_ref, gpost_ref, win_ref, wout_ref, o_ref):
    x = h_ref[...]
    xn = _rms_rows(x, gpre_ref[...]).astype(BF16)
    acc = jnp.zeros(x.shape, F32)
    c0 = 0
    for ck in FF_CHUNKS:
        a = jnp.dot(xn, win_ref[:, c0:c0 + ck], preferred_element_type=F32)
        b = jnp.dot(xn, win_ref[:, D_FF + c0:D_FF + c0 + ck], preferred_element_type=F32)
        g = (a * jax.nn.sigmoid(a) * b).astype(BF16)
        acc = acc + jnp.dot(g, wout_ref[c0:c0 + ck, :], preferred_element_type=F32)
        c0 += ck
    o_ref[...] = x + 0.5 * _rms_rows(acc, gpost_ref[...])


def _ffn(h, gpre, gpost, w_in, w_out):
    tp = h.shape[0]
    return pl.pallas_call(
        _ffn_body,
        grid=(tp // FFN_TM,),
        in_specs=[
            pl.BlockSpec((FFN_TM, D_MODEL), lambda i: (i, 0)),
            _const_spec((1, D_MODEL)),
            _const_spec((1, D_MODEL)),
            pl.BlockSpec((D_MODEL, 2 * D_FF), lambda i: (0, 0), pipeline_mode=pl.Buffered(1)),
            pl.BlockSpec((D_FF, D_MODEL), lambda i: (0, 0), pipeline_mode=pl.Buffered(1)),
        ],
        out_specs=pl.BlockSpec((FFN_TM, D_MODEL), lambda i: (i, 0)),
        out_shape=jax.ShapeDtypeStruct(h.shape, F32),
        compiler_params=_cparams(("arbitrary",)),
        name="ffn",
    )(h, gpre.reshape(1, -1), gpost.reshape(1, -1), w_in, w_out)


def _head_norm_rope(blk, gain, cos, sin):
    y = blk * lax.rsqrt(jnp.mean(blk * blk, axis=0, keepdims=True) + NORM_EPS) * gain
    x1 = y[:ROPE_AXIS]
    x2 = y[ROPE_AXIS:]
    return jnp.concatenate([x1 * cos - x2 * sin, x1 * sin + x2 * cos], axis=0)


def _qkv_body(h_ref, g_ref, wt_ref, qg_ref, kg_ref, cos_ref, sin_ref, qt_ref, k_ref, vt_ref):
    hn = _rms_rows(h_ref[...], g_ref[...]).astype(BF16)
    qkv_t = lax.dot_general(wt_ref[...], hn, (((1,), (1,)), ((), ())),
                            preferred_element_type=F32)
    cos = cos_ref[...]
    sin = sin_ref[...]
    qg = qg_ref[...]
    kg = kg_ref[...]
    for hq in range(N_Q_HEADS):
        blk = qkv_t[hq * HEAD_DIM:(hq + 1) * HEAD_DIM]
        q = _head_norm_rope(blk, qg, cos, sin) * (HEAD_DIM ** -0.5 * LOG2E)
        qt_ref[hq * HEAD_DIM:(hq + 1) * HEAD_DIM, :] = q.astype(BF16)
    k_off = N_Q_HEADS * HEAD_DIM
    k_heads = []
    for hk in range(N_KV_HEADS):
        blk = qkv_t[k_off + hk * HEAD_DIM:k_off + (hk + 1) * HEAD_DIM]
        k_heads.append(_head_norm_rope(blk, kg, cos, sin))
    k_t = jnp.concatenate(k_heads, axis=0)
    k_ref[0] = k_t.T.astype(BF16)
    v_off = k_off + N_KV_HEADS * HEAD_DIM
    tm = hn.shape[0]
    ones_rows = (lax.broadcasted_iota(jnp.int32, (V_ROWS - HEAD_DIM, tm), 0) == 0).astype(BF16)
    for hk in range(N_KV_HEADS):
        v = qkv_t[v_off + hk * HEAD_DIM:v_off + (hk + 1) * HEAD_DIM].astype(BF16)
        vt_ref[hk, 0] = jnp.concatenate([v, ones_rows], axis=0)


def _attn_qkv(h, g, w_t, q_gain, k_gain, cos_t, sin_t):
    tp = h.shape[0]
    nt = tp // ATT_TK
    return pl.pallas_call(
        _qkv_body,
        grid=(nt,),
        in_specs=[
            pl.BlockSpec((ATT_TK, D_MODEL), lambda i: (i, 0)),
            _const_spec((1, D_MODEL)),
            _const_spec((QKV_WIDTH, D_MODEL)),
            _const_spec((HEAD_DIM, 1)),
            _const_spec((HEAD_DIM, 1)),
            pl.BlockSpec((ROPE_AXIS, ATT_TK), lambda i: (0, i)),
            pl.BlockSpec((ROPE_AXIS, ATT_TK), lambda i: (0, i)),
        ],
        out_specs=[
            pl.BlockSpec((N_Q_HEADS * HEAD_DIM, ATT_TK), lambda i: (0, i)),
            pl.BlockSpec((1, ATT_TK, N_KV_HEADS * HEAD_DIM), lambda i: (i, 0, 0)),
            pl.BlockSpec((N_KV_HEADS, 1, V_ROWS, ATT_TK), lambda i: (0, i, 0, 0)),
        ],
        out_shape=[
            jax.ShapeDtypeStruct((N_Q_HEADS * HEAD_DIM, tp), BF16),
            jax.ShapeDtypeStruct((nt, ATT_TK, N_KV_HEADS * HEAD_DIM), BF16),
            jax.ShapeDtypeStruct((N_KV_HEADS, nt, V_ROWS, ATT_TK), BF16),
        ],
        compiler_params=_cparams(("arbitrary",)),
        name="attn_qkv",
    )(h, g.reshape(1, -1), w_t, q_gain.reshape(-1, 1), k_gain.reshape(-1, 1), cos_t, sin_t)


def _attn_body(qt_ref, k_ref, vt_ref, o_ref, qaug_ref, sa_ref, sb_ref, *, n_chunks, n_valid_last):
    kvh = pl.program_id(1)
    tq = TILE
    qaug_ref[...] = jnp.zeros(qaug_ref.shape, BF16)
    row0 = pl.multiple_of(kvh * HEAD_DIM, HEAD_DIM)
    for g in range(GROUP):
        qaug_ref[g, pl.ds(row0, HEAD_DIM), :] = qt_ref[g * HEAD_DIM:(g + 1) * HEAD_DIM, :]

    def scores(c, g, s_ref):
        s = jnp.dot(k_ref[c], qaug_ref[g], preferred_element_type=F32)
        s_ref[g] = s
        return jnp.max(s, axis=0, keepdims=True)

    def softmax_pv(c, g, s_ref, cmax, m, acc, masked):
        s = s_ref[g]
        if masked:
            rows = lax.broadcasted_iota(jnp.int32, s.shape, 0)
            s = jnp.where(rows < n_valid_last, s, NEG_BIG)
            cmax = jnp.max(s, axis=0, keepdims=True)
        m_new = jnp.maximum(m, cmax)
        alpha = jnp.exp2(m - m_new)
        p = jnp.exp2(s - m_new).astype(BF16)
        acc = alpha * acc + jnp.dot(vt_ref[0, c], p, preferred_element_type=F32)
        return m_new, acc

    def step(c_next, next_ref, c_cur, cur_ref, cmaxes, state):
        new_cmax, new_state = [], []
        for g in range(GROUP):
            new_cmax.append(scores(c_next, g, next_ref))
            new_state.append(softmax_pv(c_cur, g, cur_ref, cmaxes[g], *state[g], masked=False))
        return tuple(new_cmax), tuple(new_state)

    def pair(i, carry):
        cmaxes, state = carry
        cmaxes, state = step(2 * i + 1, sb_ref, 2 * i, sa_ref, cmaxes, state)
        return step(2 * i + 2, sa_ref, 2 * i + 1, sb_ref, cmaxes, state)

    state = tuple((jnp.full((1, tq), NEG_BIG, F32), jnp.zeros((V_ROWS, tq), F32)) for _ in range(GROUP))
    cmaxes = tuple(scores(0, g, sa_ref) for g in range(GROUP))
    cmaxes, state = lax.fori_loop(0, (n_chunks - 1) // 2, pair, (cmaxes, state))
    state = tuple(softmax_pv(n_chunks - 1, g, sa_ref, cmaxes[g], *state[g], masked=True)
                  for g in range(GROUP))
    for g in range(GROUP):
        acc = state[g][1]
        out_t = acc[:HEAD_DIM] / acc[HEAD_DIM:HEAD_DIM + 1]
        o_ref[:, g * HEAD_DIM:(g + 1) * HEAD_DIM] = out_t.T.astype(BF16)


def _attention(q_t, k, v_t, batch, n_valid_last):
    tp = q_t.shape[1]
    n_chunks = tp // ATT_TK // batch
    n_q = tp // TILE // batch
    body = functools.partial(_attn_body, n_chunks=n_chunks, n_valid_last=n_valid_last)
    return pl.pallas_call(
        body,
        grid=(batch, N_KV_HEADS, n_q),
        in_specs=[
            pl.BlockSpec((GROUP * HEAD_DIM, TILE), lambda b, h, i: (h, b * n_q + i)),
            pl.BlockSpec((n_chunks, ATT_TK, N_KV_HEADS * HEAD_DIM), lambda b, h, i: (b, 0, 0)),
            pl.BlockSpec((1, n_chunks, V_ROWS, ATT_TK), lambda b, h, i: (h, b, 0, 0)),
        ],
        out_specs=pl.BlockSpec((TILE, GROUP * HEAD_DIM), lambda b, h, i: (b * n_q + i, h)),
        out_shape=jax.ShapeDtypeStruct((tp, N_Q_HEADS * HEAD_DIM), BF16),
        scratch_shapes=[pltpu.VMEM((GROUP, N_KV_HEADS * HEAD_DIM, TILE), BF16),
                        pltpu.VMEM((GROUP, ATT_TK, TILE), F32),
                        pltpu.VMEM((GROUP, ATT_TK, TILE), F32)],
        compiler_params=_cparams(("arbitrary", "arbitrary", "arbitrary")),
        name="attention",
    )(q_t, k, v_t)


def _proj_res_body(a_ref, h_ref, w_ref, g_ref, o_ref):
    m = jnp.dot(a_ref[...], w_ref[...], preferred_element_type=F32)
    o_ref[...] = h_ref[...] + _rms_rows(m, g_ref[...])


def _proj_residual(a, h, w, g):
    tp, kdim = a.shape
    return pl.pallas_call(
        _proj_res_body,
        grid=(tp // FFN_TM,),
        in_specs=[
            pl.BlockSpec((FFN_TM, kdim), lambda i: (i, 0)),
            pl.BlockSpec((FFN_TM, D_MODEL), lambda i: (i, 0)),
            _const_spec((kdim, D_MODEL)),
            _const_spec((1, D_MODEL)),
        ],
        out_specs=pl.BlockSpec((FFN_TM, D_MODEL), lambda i: (i, 0)),
        out_shape=jax.ShapeDtypeStruct(h.shape, F32),
        compiler_params=_cparams(("arbitrary",)),
        name="proj_residual",
    )(a, h, w, g.reshape(1, -1))


def _rec_in_body(h_ref, g_ref, w_ref, gate_ref, xr_ref):
    hn = _rms_rows(h_ref[...], g_ref[...]).astype(BF16)
    u = jnp.dot(hn, w_ref[...], preferred_element_type=F32)
    x = u[:, :D_RNN]
    cdf = 0.5 * (1.0 + jnp.tanh(0.7978845608028654 * (x + 0.044715 * (x * x * x))))
    gate_ref[...] = x * cdf
    xr_ref[...] = u[:, D_RNN:]


def _rec_in(h, g, w):
    tp = h.shape[0]
    spec = pl.BlockSpec((FFN_TM, D_MODEL), lambda i: (i, 0))
    return pl.pallas_call(
        _rec_in_body,
        grid=(tp // FFN_TM,),
        in_specs=[spec, _const_spec((1, D_MODEL)), _const_spec((D_MODEL, 2 * D_RNN))],
        out_specs=[spec, spec],
        out_shape=[jax.ShapeDtypeStruct((tp, D_RNN), F32)] * 2,
        compiler_params=_cparams(("arbitrary",)),
        name="rec_in",
    )(h, g.reshape(1, -1), w)


def _conv_gates(x, prev8, next8, has_prev, has_next, last_row, n_valid_rows,
                cw_ref, cb_ref, wr_ref, wi_ref, br_ref, bi_ref, lam_ref):
    tt = x.shape[0]
    rows = lax.broadcasted_iota(jnp.int32, x.shape, 0)
    p7 = jnp.where(has_prev, prev8[7:8], 0.0)
    p6 = jnp.where(has_prev, prev8[6:7], 0.0)
    n0 = jnp.where(has_next, next8[0:1], 0.0)
    xm1 = jnp.where(rows == 0, p7, pltpu.roll(x, 1, axis=0))
    xm2 = jnp.where(rows == 0, p6, jnp.where(rows == 1, p7, pltpu.roll(x, 2, axis=0)))
    xp1 = jnp.where(rows == last_row, n0, pltpu.roll(x, tt - 1, axis=0))
    cw = cw_ref[...]
    xc = cb_ref[...] + xm2 * cw[0:1] + xm1 * cw[1:2] + x * cw[2:3] + xp1 * cw[3:4]
    softplus = jnp.logaddexp(-lam_ref[...], 0.0)
    r_parts = []
    i_parts = []
    for blk in range(N_RG_BLOCKS):
        xb = xc[:, blk * RG_BW:(blk + 1) * RG_BW].astype(BF16)
        r_parts.append(jnp.dot(xb, wr_ref[blk], preferred_element_type=F32))
        i_parts.append(jnp.dot(xb, wi_ref[blk], preferred_element_type=F32))
    r = jax.nn.sigmoid(jnp.concatenate(r_parts, axis=1) + br_ref[...])
    i = jax.nn.sigmoid(jnp.concatenate(i_parts, axis=1) + bi_ref[...])
    log_a = -RG_C * r * softplus
    a = jnp.exp(log_a)
    b = jnp.sqrt(1.0 - a * a) * i * xc
    valid = rows < n_valid_rows
    return jnp.where(valid, a, 1.0), jnp.where(valid, b, 0.0)


def _scan_fwd_body(x_ref, prev_ref, next_ref, cw_ref, cb_ref, wr_ref, wi_ref, br_ref, bi_ref,
                   lam_ref, hf_ref, a_s, b_s, carry_s, *, n_tiles):
    j = pl.program_id(1)
    is_tail = j == 0
    a, b = _conv_gates(
        x_ref[...], prev_ref[...], next_ref[...],
        has_prev=j > 0, has_next=j < n_tiles - 1,
        last_row=jnp.where(is_tail, N_META - 1, TILE - 1),
        n_valid_rows=jnp.where(is_tail, N_META, TILE),
        cw_ref=cw_ref, cb_ref=cb_ref, wr_ref=wr_ref, wi_ref=wi_ref,
        br_ref=br_ref, bi_ref=bi_ref, lam_ref=lam_ref)
    a_s[...] = a
    b_s[...] = b

    @pl.when(j == 0)
    def _():
        carry_s[...] = jnp.zeros(carry_s.shape, F32)

    def step(r, c):
        c = a_s[pl.ds(r, 1), :] * c + b_s[pl.ds(r, 1), :]
        hf_ref[pl.ds(r, 1), :] = c
        return c

    carry_s[...] = lax.fori_loop(0, TILE, step, carry_s[...], unroll=8)


def _scan_bwd_body(x_ref, prev_ref, next_ref, cw_ref, cb_ref, wr_ref, wi_ref, br_ref, bi_ref,
                   lam_ref, hf_ref, gate_ref, h_ref, wout_ref, g_ref, o_ref,
                   a_s, b_s, hb_s, carry_s, *, n_tiles):
    j = pl.program_id(1)
    is_tail = j == n_tiles - 1
    a, b = _conv_gates(
        x_ref[...], prev_ref[...], next_ref[...],
        has_prev=j < n_tiles - 1, has_next=j > 0,
        last_row=jnp.where(is_tail, N_META - 1, TILE - 1),
        n_valid_rows=jnp.where(is_tail, N_META, TILE),
        cw_ref=cw_ref, cb_ref=cb_ref, wr_ref=wr_ref, wi_ref=wi_ref,
        br_ref=br_ref, bi_ref=bi_ref, lam_ref=lam_ref)
    a_s[...] = a
    b_s[...] = b

    @pl.when(j == 0)
    def _():
        carry_s[...] = jnp.zeros(carry_s.shape, F32)

    def step(t, c):
        r = TILE - 1 - t
        c = a_s[pl.ds(r, 1), :] * c + b_s[pl.ds(r, 1), :]
        hb_s[pl.ds(r, 1), :] = c
        return c

    carry_s[...] = lax.fori_loop(0, TILE, step, carry_s[...], unroll=8)
    y = ((hf_ref[...] + hb_s[...]) * gate_ref[...]).astype(BF16)
    m = jnp.dot(y, wout_ref[...], preferred_element_type=F32)
    o_ref[...] = h_ref[...] + _rms_rows(m, g_ref[...])


def _scan_specs(batch, n_tiles, reverse):
    n_x = n_tiles - 1
    rpt = TILE // 8

    def logical(j):
        return (n_tiles - 1 - j) if reverse else j

    def phys_tile(b, j):
        lj = logical(j)
        return b * n_tiles + jnp.where(lj == 0, n_x, lj - 1)

    def prev_blk(b, j):
        lj = logical(j)
        base = b * n_tiles * rpt
        idx = jnp.where(lj <= 1, n_x * rpt + 1, (lj - 1) * rpt - 1)
        return base + idx

    def next_blk(b, j):
        lj = logical(j)
        base = b * n_tiles * rpt
        idx = jnp.where(lj == 0, 0, jnp.minimum(lj, n_x - 1) * rpt)
        return base + idx

    tile_spec = pl.BlockSpec((TILE, D_RNN), lambda b, j: (phys_tile(b, j), 0))
    prev_spec = pl.BlockSpec((8, D_RNN), lambda b, j: (prev_blk(b, j), 0))
    next_spec = pl.BlockSpec((8, D_RNN), lambda b, j: (next_blk(b, j), 0))
    return tile_spec, prev_spec, next_spec


def _gate_specs():
    return [
        _const_spec((4, D_RNN)),
        _const_spec((1, D_RNN)),
        _const_spec((N_RG_BLOCKS, RG_BW, RG_BW)),
        _const_spec((N_RG_BLOCKS, RG_BW, RG_BW)),
        _const_spec((1, D_RNN)),
        _const_spec((1, D_RNN)),
        _const_spec((1, D_RNN)),
    ]


def _scan_fwd(xr, batch, conv_w, conv_b, wr, wi, br, bi, lam):
    tp = xr.shape[0]
    n_tiles = tp // TILE // batch
    tile_spec, prev_spec, next_spec = _scan_specs(batch, n_tiles, reverse=False)
    return pl.pallas_call(
        functools.partial(_scan_fwd_body, n_tiles=n_tiles),
        grid=(batch, n_tiles),
        in_specs=[tile_spec, prev_spec, next_spec] + _gate_specs(),
        out_specs=tile_spec,
        out_shape=jax.ShapeDtypeStruct((tp, D_RNN), F32),
        scratch_shapes=[pltpu.VMEM((TILE, D_RNN), F32), pltpu.VMEM((TILE, D_RNN), F32),
                        pltpu.VMEM((1, D_RNN), F32)],
        compiler_params=_cparams(("arbitrary", "arbitrary")),
        name="scan_fwd",
    )(xr, xr, xr, conv_w, conv_b, wr, wi, br, bi, lam)


def _scan_bwd(xr, batch, conv_w, conv_b, wr, wi, br, bi, lam, hf, gate, h, w_out, g):
    tp = xr.shape[0]
    n_tiles = tp // TILE // batch
    tile_spec, prev_spec, next_spec = _scan_specs(batch, n_tiles, reverse=True)
    return pl.pallas_call(
        functools.partial(_scan_bwd_body, n_tiles=n_tiles),
        grid=(batch, n_tiles),
        in_specs=[tile_spec, prev_spec, next_spec] + _gate_specs()
        + [tile_spec, tile_spec, tile_spec, _const_spec((D_RNN, D_MODEL)), _const_spec((1, D_MODEL))],
        out_specs=tile_spec,
        out_shape=jax.ShapeDtypeStruct((tp, D_MODEL), F32),
        scratch_shapes=[pltpu.VMEM((TILE, D_RNN), F32), pltpu.VMEM((TILE, D_RNN), F32),
                        pltpu.VMEM((TILE, D_RNN), F32), pltpu.VMEM((1, D_RNN), F32)],
        compiler_params=_cparams(("arbitrary", "arbitrary")),
        name="scan_bwd",
    )(xr, xr, xr, conv_w, conv_b, wr, wi, br, bi, lam, hf, gate, h, w_out, g.reshape(1, -1))


def _rope_tables_t(n_tok, lp, batch):
    n = jnp.arange(n_tok, dtype=jnp.int32)
    row = (n // GRID_W).astype(F32)
    col = (n % GRID_W).astype(F32)
    zeros = jnp.zeros((lp - n_tok,), F32)
    row = jnp.concatenate([row, zeros])
    col = jnp.concatenate([col, zeros])
    inv = 1.0 / (ROPE_THETA ** (jnp.arange(0, ROPE_AXIS, 2, dtype=F32) / ROPE_AXIS))
    theta_t = jnp.concatenate([inv[:, None] * row[None, :], inv[:, None] * col[None, :]], axis=0)
    theta_t = jnp.tile(theta_t, (1, batch))
    return jnp.cos(theta_t), jnp.sin(theta_t)


def kernel(x, meta_tokens, norm_gains, ffn_w_in, ffn_w_out, attn_w_qkv, attn_q_gain, attn_k_gain,
           attn_w_o, rec_w_in, rec_conv_w, rec_conv_b, rec_gate_w, rec_gate_b, rec_lambda, rec_w_out):
    batch, n_tok, d = x.shape
    assert d == D_MODEL and n_tok % TILE == 0 and N_META <= TILE
    lp = n_tok + TILE
    assert (batch * lp) % FFN_TM == 0 and lp % ATT_TK == 0
    depth = norm_gains.shape[0]

    meta = jnp.broadcast_to(meta_tokens.astype(x.dtype)[None], (batch, N_META, d))
    pad = jnp.zeros((batch, TILE - N_META, d), x.dtype)
    h = jnp.concatenate([x, meta, pad], axis=1).reshape(batch * lp, d)
    cos_t, sin_t = _rope_tables_t(n_tok, lp, batch)

    for layer in range(depth):
        g = norm_gains[layer]
        h = _ffn(h, g[0], g[1], ffn_w_in[layer, 0].astype(BF16), ffn_w_out[layer, 0].astype(BF16))
        j = layer // 2
        if layer % 2 == 0:
            q_t, k, v_t = _attn_qkv(h, g[2], attn_w_qkv[j].T.astype(BF16), attn_q_gain[j],
                                    attn_k_gain[j], cos_t, sin_t)
            o = _attention(q_t, k, v_t, batch, n_tok + N_META - (lp - ATT_TK))
            h = _proj_residual(o, h, attn_w_o[j].astype(BF16), g[3])
        else:
            gate, xr = _rec_in(h, g[2], rec_w_in[j].astype(BF16))
            gw = rec_gate_w[j].astype(BF16)
            gb = rec_gate_b[j]
            lam = rec_lambda[j]
            cw = rec_conv_w[j]
            cb = rec_conv_b[j].reshape(1, -1)
            hf = _scan_fwd(xr, batch, cw, cb, gw[0, 0], gw[0, 1], gb[0, 0].reshape(1, -1),
                           gb[0, 1].reshape(1, -1), lam[0].reshape(1, -1))
            h = _scan_bwd(xr, batch, cw, cb, gw[1, 0], gw[1, 1], gb[1, 0].reshape(1, -1),
                          gb[1, 1].reshape(1, -1), lam[1].reshape(1, -1), hf, gate, h,
                          rec_w_out[j].astype(BF16), g[3])
        h = _ffn(h, g[4], g[5], ffn_w_in[layer, 1].astype(BF16), ffn_w_out[layer, 1].astype(BF16))
    return h.reshape(batch, lp, d)[:, :n_tok]
```

```python
import functools

import jax
import jax.numpy as jnp
from jax import lax
from jax.experimental import pallas as pl
from jax.experimental.pallas import tpu as pltpu

D_MODEL = 1024
N_META = 16
GRID_W = 64
HEAD_DIM = 64
N_Q_HEADS = 16
N_KV_HEADS = 4
GROUP = 4
QKV_WIDTH = 1536
ROPE_AXIS = 32
ROPE_THETA = 10000.0
D_RNN = 1024
N_RG_BLOCKS = 4
RG_BW = 256
RG_C = 8.0
D_FF = 2816
NORM_EPS = 1e-6

VMEM_LIMIT_BYTES = 56 * 1024 * 1024

TILE = 256
ATT_TK = 1024
ATT_NQ = 13
LOG2E = 1.4426950408889634
FFN_TM = 512
V_ROWS = 80
NEG_BIG = -1e30

BF16 = jnp.bfloat16
F32 = jnp.float32


def _cparams(sem):
    return pltpu.CompilerParams(dimension_semantics=sem, vmem_limit_bytes=VMEM_LIMIT_BYTES)


def _rms_rows(x, g):
    return x * lax.rsqrt(jnp.mean(x * x, axis=-1, keepdims=True) + NORM_EPS) * g


def _const_spec(shape):
    zeros = (0,) * len(shape)
    return pl.BlockSpec(shape, lambda *_: zeros)


FF_CHUNKS = (768, 768, 768, 512)


def _ffn_body(h_ref, gpre_ref, gpost_ref, win_ref, wout_ref, o_ref):
    x = h_ref[...]
    xn = _rms_rows(x, gpre_ref[...]).astype(BF16)
    acc = jnp.zeros(x.shape, F32)
    c0 = 0
    for ck in FF_CHUNKS:
        a = jnp.dot(xn, win_ref[:, c0:c0 + ck], preferred_element_type=F32)
        b = jnp.dot(xn, win_ref[:, D_FF + c0:D_FF + c0 + ck], preferred_element_type=F32)
        g = (a * jax.nn.sigmoid(a) * b).astype(BF16)
        acc = acc + jnp.dot(g, wout_ref[c0:c0 + ck, :], preferred_element_type=F32)
        c0 += ck
    o_ref[...] = x + 0.5 * _rms_rows(acc, gpost_ref[...])


def _ffn(h, gpre, gpost, w_in, w_out):
    tp = h.shape[0]
    return pl.pallas_call(
        _ffn_body,
        grid=(tp // FFN_TM,),
        in_specs=[
            pl.BlockSpec((FFN_TM, D_MODEL), lambda i: (i, 0)),
            _const_spec((1, D_MODEL)),
            _const_spec((1, D_MODEL)),
            pl.BlockSpec((D_MODEL, 2 * D_FF), lambda i: (0, 0), pipeline_mode=pl.Buffered(1)),
            pl.BlockSpec((D_FF, D_MODEL), lambda i: (0, 0), pipeline_mode=pl.Buffered(1)),
        ],
        out_specs=pl.BlockSpec((FFN_TM, D_MODEL), lambda i: (i, 0)),
        out_shape=jax.ShapeDtypeStruct(h.shape, F32),
        compiler_params=_cparams(("arbitrary",)),
        name="ffn",
    )(h, gpre.reshape(1, -1), gpost.reshape(1, -1), w_in, w_out)


def _head_norm_rope(blk, gain, cos, sin):
    y = blk * lax.rsqrt(jnp.mean(blk * blk, axis=0, keepdims=True) + NORM_EPS) * gain
    x1 = y[:ROPE_AXIS]
    x2 = y[ROPE_AXIS:]
    return jnp.concatenate([x1 * cos - x2 * sin, x1 * sin + x2 * cos], axis=0)


def _qkv_body(h_ref, g_ref, wt_ref, qg_ref, kg_ref, cos_ref, sin_ref, qt_ref, k_ref, vt_ref):
    hn = _rms_rows(h_ref[...], g_ref[...]).astype(BF16)
    qkv_t = lax.dot_general(wt_ref[...], hn, (((1,), (1,)), ((), ())),
                            preferred_element_type=F32)
    cos = cos_ref[...]
    sin = sin_ref[...]
    qg = qg_ref[...]
    kg = kg_ref[...]
    for hq in range(N_Q_HEADS):
        blk = qkv_t[hq * HEAD_DIM:(hq + 1) * HEAD_DIM]
        q = _head_norm_rope(blk, qg, cos, sin) * (HEAD_DIM ** -0.5 * LOG2E)
        qt_ref[hq * HEAD_DIM:(hq + 1) * HEAD_DIM, :] = q.astype(BF16)
    k_off = N_Q_HEADS * HEAD_DIM
    k_heads = []
    for hk in range(N_KV_HEADS):
        blk = qkv_t[k_off + hk * HEAD_DIM:k_off + (hk + 1) * HEAD_DIM]
        k_heads.append(_head_norm_rope(blk, kg, cos, sin))
    k_t = jnp.concatenate(k_heads, axis=0)
    k_ref[0] = k_t.T.astype(BF16)
    v_off = k_off + N_KV_HEADS * HEAD_DIM
    tm = hn.shape[0]
    ones_rows = (lax.broadcasted_iota(jnp.int32, (V_ROWS - HEAD_DIM, tm), 0) == 0).astype(BF16)
    for hk in range(N_KV_HEADS):
        v = qkv_t[v_off + hk * HEAD_DIM:v_off + (hk + 1) * HEAD_DIM].astype(BF16)
        vt_ref[hk, 0] = jnp.concatenate([v, ones_rows], axis=0)


def _attn_qkv(h, g, w_t, q_gain, k_gain, cos_t, sin_t, batch):
    tp = h.shape[0]
    nt = tp // TILE
    tiles_per_batch = nt // batch
    tpc = ATT_TK // TILE
    vchunks = pl.cdiv(tiles_per_batch, tpc)

    def vt_index(i):
        b = i // tiles_per_batch
        t = i % tiles_per_batch
        return (0, b * vchunks + t // tpc, 0, t % tpc)

    return pl.pallas_call(
        _qkv_body,
        grid=(nt,),
        in_specs=[
            pl.BlockSpec((TILE, D_MODEL), lambda i: (i, 0)),
            _const_spec((1, D_MODEL)),
            _const_spec((QKV_WIDTH, D_MODEL)),
            _const_spec((HEAD_DIM, 1)),
            _const_spec((HEAD_DIM, 1)),
            pl.BlockSpec((ROPE_AXIS, TILE), lambda i: (0, i)),
            pl.BlockSpec((ROPE_AXIS, TILE), lambda i: (0, i)),
        ],
        out_specs=[
            pl.BlockSpec((N_Q_HEADS * HEAD_DIM, TILE), lambda i: (0, i)),
            pl.BlockSpec((1, TILE, N_KV_HEADS * HEAD_DIM), lambda i: (i, 0, 0)),
            pl.BlockSpec((N_KV_HEADS, 1, V_ROWS, TILE), vt_index),
        ],
        out_shape=[
            jax.ShapeDtypeStruct((N_Q_HEADS * HEAD_DIM, tp), BF16),
            jax.ShapeDtypeStruct((nt, TILE, N_KV_HEADS * HEAD_DIM), BF16),
            jax.ShapeDtypeStruct((N_KV_HEADS, batch * vchunks, V_ROWS, ATT_TK), BF16),
        ],
        compiler_params=_cparams(("arbitrary",)),
        name="attn_qkv",
    )(h, g.reshape(1, -1), w_t, q_gain.reshape(-1, 1), k_gain.reshape(-1, 1), cos_t, sin_t)


def _attn_body(qt_ref, k_ref, vt_ref, o_ref, qaug_ref, sa_ref, sb_ref, *, n_chunks, n_q, meta_rows):
    kvh = pl.program_id(1)
    tq = TILE
    tpc = ATT_TK // TILE
    meta_tile = n_chunks * tpc
    n_steps = n_q * n_chunks
    assert n_chunks % 2 == 0

    qaug_ref[...] = jnp.zeros(qaug_ref.shape, BF16)
    row0 = pl.multiple_of(kvh * HEAD_DIM, HEAD_DIM)
    for j in range(n_q):
        for g in range(GROUP):
            qaug_ref[j * GROUP + g, pl.ds(row0, HEAD_DIM), :] = (
                qt_ref[g * HEAD_DIM:(g + 1) * HEAD_DIM, j * tq:(j + 1) * tq])

    def scores(t, g, s_ref):
        qi = t // n_chunks
        c = t % n_chunks
        kc = k_ref[pl.ds(c * tpc, tpc)].reshape(ATT_TK, N_KV_HEADS * HEAD_DIM)
        s = jnp.dot(kc, qaug_ref[qi * GROUP + g], preferred_element_type=F32)
        s_ref[g] = s
        return jnp.max(s, axis=0, keepdims=True)

    def softmax_pv(c, g, s_ref, cmax, m, acc):
        m_new = jnp.maximum(m, cmax)
        alpha = jnp.exp2(m - m_new)
        p = jnp.exp2(s_ref[g] - m_new).astype(BF16)
        acc = alpha * acc + jnp.dot(vt_ref[0, c], p, preferred_element_type=F32)
        return m_new, acc

    def finalize(qi, state):
        k_meta = k_ref[meta_tile, 0:128, :]
        v_meta = vt_ref[0, n_chunks, :, 0:128]
        for g in range(GROUP):
            m, acc = state[g]
            s = jnp.dot(k_meta, qaug_ref[qi * GROUP + g], preferred_element_type=F32)
            rows = lax.broadcasted_iota(jnp.int32, s.shape, 0)
            s = jnp.where(rows < meta_rows, s, NEG_BIG)
            m_new = jnp.maximum(m, jnp.max(s, axis=0, keepdims=True))
            p = jnp.exp2(s - m_new).astype(BF16)
            acc = jnp.exp2(m - m_new) * acc + jnp.dot(v_meta, p, preferred_element_type=F32)
            out_t = acc[:HEAD_DIM] / acc[HEAD_DIM:HEAD_DIM + 1]
            o_ref[pl.ds(pl.multiple_of(qi * tq, tq), tq), g * HEAD_DIM:(g + 1) * HEAD_DIM] = (
                out_t.T.astype(BF16))

    def step(t_next, next_ref, cur_ref, cmaxes, state, reset):
        c_cur = (t_next - 1) % n_chunks
        new_cmax, new_state = [], []
        for g in range(GROUP):
            m, acc = state[g]
            if reset:
                first = c_cur == 0
                m = jnp.where(first, NEG_BIG, m)
                acc = jnp.where(first, 0.0, acc)
            new_cmax.append(scores(t_next, g, next_ref))
            new_state.append(softmax_pv(c_cur, g, cur_ref, cmaxes[g], m, acc))
        return tuple(new_cmax), tuple(new_state)

    def pair(i, carry):
        cmaxes, state = carry
        cmaxes, state = step(2 * i + 1, sb_ref, sa_ref, cmaxes, state, reset=True)
        cmaxes, state = step(2 * i + 2, sa_ref, sb_ref, cmaxes, state, reset=False)

        @pl.when((2 * i + 1) % n_chunks == n_chunks - 1)
        def _():
            finalize((2 * i + 1) // n_chunks, state)

        return cmaxes, state

    state = tuple((jnp.full((1, tq), NEG_BIG, F32), jnp.zeros((V_ROWS, tq), F32)) for _ in range(GROUP))
    cmaxes = tuple(scores(0, g, sa_ref) for g in range(GROUP))
    cmaxes, state = lax.fori_loop(0, n_steps // 2 - 1, pair, (cmaxes, state))
    cmaxes, state = step(n_steps - 1, sb_ref, sa_ref, cmaxes, state, reset=True)
    state = tuple(softmax_pv(n_chunks - 1, g, sb_ref, cmaxes[g], *state[g]) for g in range(GROUP))
    finalize(n_q - 1, state)


def _attention(q_t, k, v_t, batch, n_tok, lp):
    tp = q_t.shape[1]
    n_chunks = n_tok // ATT_TK
    tiles_per_batch = lp // TILE
    n_q = max(d for d in range(1, ATT_NQ + 1) if tiles_per_batch % d == 0)
    q_steps = tiles_per_batch // n_q
    vchunks = v_t.shape[1] // batch
    body = functools.partial(_attn_body, n_chunks=n_chunks, n_q=n_q, meta_rows=N_META)
    return pl.pallas_call(
        body,
        grid=(batch, N_KV_HEADS, q_steps),
        in_specs=[
            pl.BlockSpec((GROUP * HEAD_DIM, n_q * TILE), lambda b, h, i: (h, b * q_steps + i)),
            pl.BlockSpec((tiles_per_batch, TILE, N_KV_HEADS * HEAD_DIM), lambda b, h, i: (b, 0, 0)),
            pl.BlockSpec((1, vchunks, V_ROWS, ATT_TK), lambda b, h, i: (h, b, 0, 0)),
        ],
        out_specs=pl.BlockSpec((n_q * TILE, GROUP * HEAD_DIM), lambda b, h, i: (b * q_steps + i, h)),
        out_shape=jax.ShapeDtypeStruct((tp, N_Q_HEADS * HEAD_DIM), BF16),
        scratch_shapes=[pltpu.VMEM((n_q * GROUP, N_KV_HEADS * HEAD_DIM, TILE), BF16),
                        pltpu.VMEM((GROUP, ATT_TK, TILE), F32),
                        pltpu.VMEM((GROUP, ATT_TK, TILE), F32)],
        compiler_params=_cparams(("arbitrary", "arbitrary", "arbitrary")),
        name="attention",
    )(q_t, k, v_t)


def _proj_res_body(a_ref, h_ref, w_ref, g_ref, o_ref):
    m = jnp.dot(a_ref[...], w_ref[...], preferred_element_type=F32)
    o_ref[...] = h_ref[...] + _rms_rows(m, g_ref[...])


def _proj_residual(a, h, w, g):
    tp, kdim = a.shape
    return pl.pallas_call(
        _proj_res_body,
        grid=(tp // FFN_TM,),
        in_specs=[
            pl.BlockSpec((FFN_TM, kdim), lambda i: (i, 0)),
            pl.BlockSpec((FFN_TM, D_MODEL), lambda i: (i, 0)),
            _const_spec((kdim, D_MODEL)),
            _const_spec((1, D_MODEL)),
        ],
        out_specs=pl.BlockSpec((FFN_TM, D_MODEL), lambda i: (i, 0)),
        out_shape=jax.ShapeDtypeStruct(h.shape, F32),
        compiler_params=_cparams(("arbitrary",)),
        name="proj_residual",
    )(a, h, w, g.reshape(1, -1))


def _rec_in_body(h_ref, g_ref, w_ref, gate_ref, xr_ref):
    hn = _rms_rows(h_ref[...], g_ref[...]).astype(BF16)
    u = jnp.dot(hn, w_ref[...], preferred_element_type=F32)
    x = u[:, :D_RNN]
    cdf = 0.5 * (1.0 + jnp.tanh(0.7978845608028654 * (x + 0.044715 * (x * x * x))))
    gate_ref[...] = x * cdf
    xr_ref[...] = u[:, D_RNN:]


def _rec_in(h, g, w):
    tp = h.shape[0]
    spec = pl.BlockSpec((FFN_TM, D_MODEL), lambda i: (i, 0))
    return pl.pallas_call(
        _rec_in_body,
        grid=(tp // FFN_TM,),
        in_specs=[spec, _const_spec((1, D_MODEL)), _const_spec((D_MODEL, 2 * D_RNN))],
        out_specs=[spec, spec],
        out_shape=[jax.ShapeDtypeStruct((tp, D_RNN), F32)] * 2,
        compiler_params=_cparams(("arbitrary",)),
        name="rec_in",
    )(h, g.reshape(1, -1), w)


def _conv_gates(x, prev8, next8, has_prev, has_next, last_row, n_valid_rows,
                cw_ref, cb_ref, wr_ref, wi_ref, br_ref, bi_ref, lam_ref):
    tt = x.shape[0]
    rows = lax.broadcasted_iota(jnp.int32, x.shape, 0)
    p7 = jnp.where(has_prev, prev8[7:8], 0.0)
    p6 = jnp.where(has_prev, prev8[6:7], 0.0)
    n0 = jnp.where(has_next, next8[0:1], 0.0)
    xm1 = jnp.where(rows == 0, p7, pltpu.roll(x, 1, axis=0))
    xm2 = jnp.where(rows == 0, p6, jnp.where(rows == 1, p7, pltpu.roll(x, 2, axis=0)))
    xp1 = jnp.where(rows == last_row, n0, pltpu.roll(x, tt - 1, axis=0))
    cw = cw_ref[...]
    xc = cb_ref[...] + xm2 * cw[0:1] + xm1 * cw[1:2] + x * cw[2:3] + xp1 * cw[3:4]
    softplus = jnp.logaddexp(-lam_ref[...], 0.0)
    r_parts = []
    i_parts = []
    for blk in range(N_RG_BLOCKS):
        xb = xc[:, blk * RG_BW:(blk + 1) * RG_BW].astype(BF16)
        r_parts.append(jnp.dot(xb, wr_ref[blk], preferred_element_type=F32))
        i_parts.append(jnp.dot(xb, wi_ref[blk], preferred_element_type=F32))
    r = jax.nn.sigmoid(jnp.concatenate(r_parts, axis=1) + br_ref[...])
    i = jax.nn.sigmoid(jnp.concatenate(i_parts, axis=1) + bi_ref[...])
    log_a = -RG_C * r * softplus
    a = jnp.exp(log_a)
    b = jnp.sqrt(1.0 - a * a) * i * xc
    valid = rows < n_valid_rows
    return jnp.where(valid, a, 1.0), jnp.where(valid, b, 0.0)


def _scan_fwd_body(x_ref, prev_ref, next_ref, cw_ref, cb_ref, wr_ref, wi_ref, br_ref, bi_ref,
                   lam_ref, hf_ref, a_s, b_s, carry_s, *, n_tiles):
    j = pl.program_id(1)
    is_tail = j == 0
    a, b = _conv_gates(
        x_ref[...], prev_ref[...], next_ref[...],
        has_prev=j > 0, has_next=j < n_tiles - 1,
        last_row=jnp.where(is_tail, N_META - 1, TILE - 1),
        n_valid_rows=jnp.where(is_tail, N_META, TILE),
        cw_ref=cw_ref, cb_ref=cb_ref, wr_ref=wr_ref, wi_ref=wi_ref,
        br_ref=br_ref, bi_ref=bi_ref, lam_ref=lam_ref)
    a_s[...] = a
    b_s[...] = b

    @pl.when(j == 0)
    def _():
        carry_s[...] = jnp.zeros(carry_s.shape, F32)

    def step(r, c):
        c = a_s[pl.ds(r, 1), :] * c + b_s[pl.ds(r, 1), :]
        hf_ref[pl.ds(r, 1), :] = c
        return c

    carry_s[...] = lax.fori_loop(0, TILE, step, carry_s[...], unroll=8)


def _scan_bwd_body(x_ref, prev_ref, next_ref, cw_ref, cb_ref, wr_ref, wi_ref, br_ref, bi_ref,
                   lam_ref, hf_ref, gate_ref, h_ref, wout_ref, g_ref, o_ref,
                   a_s, b_s, hb_s, carry_s, *, n_tiles):
    j = pl.program_id(1)
    is_tail = j == n_tiles - 1
    a, b = _conv_gates(
        x_ref[...], prev_ref[...], next_ref[...],
        has_prev=j < n_tiles - 1, has_next=j > 0,
        last_row=jnp.where(is_tail, N_META - 1, TILE - 1),
        n_valid_rows=jnp.where(is_tail, N_META, TILE),
        cw_ref=cw_ref, cb_ref=cb_ref, wr_ref=wr_ref, wi_ref=wi_ref,
        br_ref=br_ref, bi_ref=bi_ref, lam_ref=lam_ref)
    a_s[...] = a
    b_s[...] = b

    @pl.when(j == 0)
    def _():
        carry_s[...] = jnp.zeros(carry_s.shape, F32)

    def step(t, c):
        r = TILE - 1 - t
        c = a_s[pl.ds(r, 1), :] * c + b_s[pl.ds(r, 1), :]
        hb_s[pl.ds(r, 1), :] = c
        return c

    carry_s[...] = lax.fori_loop(0, TILE, step, carry_s[...], unroll=8)
    y = ((hf_ref[...] + hb_s[...]) * gate_ref[...]).astype(BF16)
    m = jnp.dot(y, wout_ref[...], preferred_element_type=F32)
    o_ref[...] = h_ref[...] + _rms_rows(m, g_ref[...])


def _scan_specs(batch, n_tiles, reverse):
    n_x = n_tiles - 1
    rpt = TILE // 8

    def logical(j):
        return (n_tiles - 1 - j) if reverse else j

    def phys_tile(b, j):
        lj = logical(j)
        return b * n_tiles + jnp.where(lj == 0, n_x, lj - 1)

    def prev_blk(b, j):
        lj = logical(j)
        base = b * n_tiles * rpt
        idx = jnp.where(lj <= 1, n_x * rpt + 1, (lj - 1) * rpt - 1)
        return base + idx

    def next_blk(b, j):
        lj = logical(j)
        base = b * n_tiles * rpt
        idx = jnp.where(lj == 0, 0, jnp.minimum(lj, n_x - 1) * rpt)
        return base + idx

    tile_spec = pl.BlockSpec((TILE, D_RNN), lambda b, j: (phys_tile(b, j), 0))
    prev_spec = pl.BlockSpec((8, D_RNN), lambda b, j: (prev_blk(b, j), 0))
    next_spec = pl.BlockSpec((8, D_RNN), lambda b, j: (next_blk(b, j), 0))
    return tile_spec, prev_spec, next_spec


def _gate_specs():
    return [
        _const_spec((4, D_RNN)),
        _const_spec((1, D_RNN)),
        _const_spec((N_RG_BLOCKS, RG_BW, RG_BW)),
        _const_spec((N_RG_BLOCKS, RG_BW, RG_BW)),
        _const_spec((1, D_RNN)),
        _const_spec((1, D_RNN)),
        _const_spec((1, D_RNN)),
    ]


def _scan_fwd(xr, batch, conv_w, conv_b, wr, wi, br, bi, lam):
    tp = xr.shape[0]
    n_tiles = tp // TILE // batch
    tile_spec, prev_spec, next_spec = _scan_specs(batch, n_tiles, reverse=False)
    return pl.pallas_call(
        functools.partial(_scan_fwd_body, n_tiles=n_tiles),
        grid=(batch, n_tiles),
        in_specs=[tile_spec, prev_spec, next_spec] + _gate_specs(),
        out_specs=tile_spec,
        out_shape=jax.ShapeDtypeStruct((tp, D_RNN), F32),
        scratch_shapes=[pltpu.VMEM((TILE, D_RNN), F32), pltpu.VMEM((TILE, D_RNN), F32),
                        pltpu.VMEM((1, D_RNN), F32)],
        compiler_params=_cparams(("arbitrary", "arbitrary")),
        name="scan_fwd",
    )(xr, xr, xr, conv_w, conv_b, wr, wi, br, bi, lam)


def _scan_bwd(xr, batch, conv_w, conv_b, wr, wi, br, bi, lam, hf, gate, h, w_out, g):
    tp = xr.shape[0]
    n_tiles = tp // TILE // batch
    tile_spec, prev_spec, next_spec = _scan_specs(batch, n_tiles, reverse=True)
    return pl.pallas_call(
        functools.partial(_scan_bwd_body, n_tiles=n_tiles),
        grid=(batch, n_tiles),
        in_specs=[tile_spec, prev_spec, next_spec] + _gate_specs()
        + [tile_spec, tile_spec, tile_spec, _const_spec((D_RNN, D_MODEL)), _const_spec((1, D_MODEL))],
        out_specs=tile_spec,
        out_shape=jax.ShapeDtypeStruct((tp, D_MODEL), F32),
        scratch_shapes=[pltpu.VMEM((TILE, D_RNN), F32), pltpu.VMEM((TILE, D_RNN), F32),
                        pltpu.VMEM((TILE, D_RNN), F32), pltpu.VMEM((1, D_RNN), F32)],
        compiler_params=_cparams(("arbitrary", "arbitrary")),
        name="scan_bwd",
    )(xr, xr, xr, conv_w, conv_b, wr, wi, br, bi, lam, hf, gate, h, w_out, g.reshape(1, -1))


def _rope_tables_t(n_tok, lp, batch):
    n = jnp.arange(n_tok, dtype=jnp.int32)
    row = (n // GRID_W).astype(F32)
    col = (n % GRID_W).astype(F32)
    zeros = jnp.zeros((lp - n_tok,), F32)
    row = jnp.concatenate([row, zeros])
    col = jnp.concatenate([col, zeros])
    inv = 1.0 / (ROPE_THETA ** (jnp.arange(0, ROPE_AXIS, 2, dtype=F32) / ROPE_AXIS))
    theta_t = jnp.concatenate([inv[:, None] * row[None, :], inv[:, None] * col[None, :]], axis=0)
    theta_t = jnp.tile(theta_t, (1, batch))
    return jnp.cos(theta_t), jnp.sin(theta_t)


def kernel(x, meta_tokens, norm_gains, ffn_w_in, ffn_w_out, attn_w_qkv, attn_q_gain, attn_k_gain,
           attn_w_o, rec_w_in, rec_conv_w, rec_conv_b, rec_gate_w, rec_gate_b, rec_lambda, rec_w_out):
    batch, n_tok, d = x.shape
    assert d == D_MODEL and n_tok % TILE == 0 and N_META <= TILE
    lp = n_tok + TILE
    assert (batch * lp) % FFN_TM == 0 and n_tok % (2 * ATT_TK) == 0 and N_META <= 128
    depth = norm_gains.shape[0]

    meta = jnp.broadcast_to(meta_tokens.astype(x.dtype)[None], (batch, N_META, d))
    pad = jnp.zeros((batch, TILE - N_META, d), x.dtype)
    h = jnp.concatenate([x, meta, pad], axis=1).reshape(batch * lp, d)
    cos_t, sin_t = _rope_tables_t(n_tok, lp, batch)

    for layer in range(depth):
        g = norm_gains[layer]
        h = _ffn(h, g[0], g[1], ffn_w_in[layer, 0].astype(BF16), ffn_w_out[layer, 0].astype(BF16))
        j = layer // 2
        if layer % 2 == 0:
            q_t, k, v_t = _attn_qkv(h, g[2], attn_w_qkv[j].T.astype(BF16), attn_q_gain[j],
                                    attn_k_gain[j], cos_t, sin_t, batch)
            o = _attention(q_t, k, v_t, batch, n_tok, lp)
            h = _proj_residual(o, h, attn_w_o[j].astype(BF16), g[3])
        else:
            gate, xr = _rec_in(h, g[2], rec_w_in[j].astype(BF16))
            gw = rec_gate_w[j].astype(BF16)
            gb = rec_gate_b[j]
            lam = rec_lambda[j]
            cw = rec_conv_w[j]
            cb = rec_conv_b[j].reshape(1, -1)
            hf = _scan_fwd(xr, batch, cw, cb, gw[0, 0], gw[0, 1], gb[0, 0].reshape(1, -1),
                           gb[0, 1].reshape(1, -1), lam[0].reshape(1, -1))
            h = _scan_bwd(xr, batch, cw, cb, gw[1, 0], gw[1, 1], gb[1, 0].reshape(1, -1),
                          gb[1, 1].reshape(1, -1), lam[1].reshape(1, -1), hf, gate, h,
                          rec_w_out[j].astype(BF16), g[3])
        h = _ffn(h, g[4], g[5], ffn_w_in[layer, 1].astype(BF16), ffn_w_out[layer, 1].astype(BF16))
    return h.reshape(batch, lp, d)[:, :n_tok]
```

```python
import functools

import jax
import jax.numpy as jnp
from jax import lax
from jax.experimental import pallas as pl
from jax.experimental.pallas import tpu as pltpu

D_MODEL = 1024
N_META = 16
GRID_W = 64
HEAD_DIM = 64
N_Q_HEADS = 16
N_KV_HEADS = 4
GROUP = 4
QKV_WIDTH = 1536
ROPE_AXIS = 32
ROPE_THETA = 10000.0
D_RNN = 1024
N_RG_BLOCKS = 4
RG_BW = 256
RG_C = 8.0
D_FF = 2816
NORM_EPS = 1e-6

VMEM_LIMIT_BYTES = 56 * 1024 * 1024

TILE = 256
ATT_TK = 1024
ATT_NQ = 13
ATT_UNROLL = 8
LOG2E = 1.4426950408889634
FFN_TM = 512
V_ROWS = 80
NEG_BIG = -1e30

BF16 = jnp.bfloat16
F32 = jnp.float32


def _cparams(sem):
    return pltpu.CompilerParams(dimension_semantics=sem, vmem_limit_bytes=VMEM_LIMIT_BYTES)


def _rms_rows(x, g):
    return x * lax.rsqrt(jnp.mean(x * x, axis=-1, keepdims=True) + NORM_EPS) * g


def _const_spec(shape):
    zeros = (0,) * len(shape)
    return pl.BlockSpec(shape, lambda *_: zeros)


FF_CHUNKS = (768, 768, 768, 512)


def _ffn_body(h_ref, gpre_ref, gpost_ref, win_ref, wout_ref, o_ref):
    x = h_ref[...]
    xn = _rms_rows(x, gpre_ref[...]).astype(BF16)
    acc = jnp.zeros(x.shape, F32)
    c0 = 0
    for ck in FF_CHUNKS:
        a = jnp.dot(xn, win_ref[:, c0:c0 + ck], preferred_element_type=F32)
        b = jnp.dot(xn, win_ref[:, D_FF + c0:D_FF + c0 + ck], preferred_element_type=F32)
        g = (a * jax.nn.sigmoid(a) * b).astype(BF16)
        acc = acc + jnp.dot(g, wout_ref[c0:c0 + ck, :], preferred_element_type=F32)
        c0 += ck
    o_ref[...] = x + 0.5 * _rms_rows(acc, gpost_ref[...])


def _ffn(h, gpre, gpost, w_in, w_out):
    tp = h.shape[0]
    return pl.pallas_call(
        _ffn_body,
        grid=(tp // FFN_TM,),
        in_specs=[
            pl.BlockSpec((FFN_TM, D_MODEL), lambda i: (i, 0)),
            _const_spec((1, D_MODEL)),
            _const_spec((1, D_MODEL)),
            pl.BlockSpec((D_MODEL, 2 * D_FF), lambda i: (0, 0), pipeline_mode=pl.Buffered(1)),
            pl.BlockSpec((D_FF, D_MODEL), lambda i: (0, 0), pipeline_mode=pl.Buffered(1)),
        ],
        out_specs=pl.BlockSpec((FFN_TM, D_MODEL), lambda i: (i, 0)),
        out_shape=jax.ShapeDtypeStruct(h.shape, F32),
        compiler_params=_cparams(("arbitrary",)),
        name="ffn",
    )(h, gpre.reshape(1, -1), gpost.reshape(1, -1), w_in, w_out)


def _head_norm_rope(blk, gain, cos, sin):
    y = blk * lax.rsqrt(jnp.mean(blk * blk, axis=0, keepdims=True) + NORM_EPS) * gain
    x1 = y[:ROPE_AXIS]
    x2 = y[ROPE_AXIS:]
    return jnp.concatenate([x1 * cos - x2 * sin, x1 * sin + x2 * cos], axis=0)


def _qkv_body(h_ref, g_ref, wt_ref, qg_ref, kg_ref, cos_ref, sin_ref, qt_ref, k_ref, vt_ref):
    hn = _rms_rows(h_ref[...], g_ref[...]).astype(BF16)
    qkv_t = lax.dot_general(wt_ref[...], hn, (((1,), (1,)), ((), ())),
                            preferred_element_type=F32)
    cos = cos_ref[...]
    sin = sin_ref[...]
    qg = qg_ref[...]
    kg = kg_ref[...]
    for hq in range(N_Q_HEADS):
        blk = qkv_t[hq * HEAD_DIM:(hq + 1) * HEAD_DIM]
        q = _head_norm_rope(blk, qg, cos, sin) * (HEAD_DIM ** -0.5 * LOG2E)
        qt_ref[hq * HEAD_DIM:(hq + 1) * HEAD_DIM, :] = q.astype(BF16)
    k_off = N_Q_HEADS * HEAD_DIM
    k_heads = []
    for hk in range(N_KV_HEADS):
        blk = qkv_t[k_off + hk * HEAD_DIM:k_off + (hk + 1) * HEAD_DIM]
        k_heads.append(_head_norm_rope(blk, kg, cos, sin))
    k_t = jnp.concatenate(k_heads, axis=0)
    k_ref[0] = k_t.T.astype(BF16)
    v_off = k_off + N_KV_HEADS * HEAD_DIM
    tm = hn.shape[0]
    ones_rows = (lax.broadcasted_iota(jnp.int32, (V_ROWS - HEAD_DIM, tm), 0) == 0).astype(BF16)
    for hk in range(N_KV_HEADS):
        v = qkv_t[v_off + hk * HEAD_DIM:v_off + (hk + 1) * HEAD_DIM].astype(BF16)
        vt_ref[hk, 0] = jnp.concatenate([v, ones_rows], axis=0)


def _attn_qkv(h, g, w_t, q_gain, k_gain, cos_t, sin_t, batch):
    tp = h.shape[0]
    nt = tp // TILE
    tiles_per_batch = nt // batch
    tpc = ATT_TK // TILE
    vchunks = pl.cdiv(tiles_per_batch, tpc)

    def vt_index(i):
        b = i // tiles_per_batch
        t = i % tiles_per_batch
        return (0, b * vchunks + t // tpc, 0, t % tpc)

    return pl.pallas_call(
        _qkv_body,
        grid=(nt,),
        in_specs=[
            pl.BlockSpec((TILE, D_MODEL), lambda i: (i, 0)),
            _const_spec((1, D_MODEL)),
            _const_spec((QKV_WIDTH, D_MODEL)),
            _const_spec((HEAD_DIM, 1)),
            _const_spec((HEAD_DIM, 1)),
            pl.BlockSpec((ROPE_AXIS, TILE), lambda i: (0, i)),
            pl.BlockSpec((ROPE_AXIS, TILE), lambda i: (0, i)),
        ],
        out_specs=[
            pl.BlockSpec((N_Q_HEADS * HEAD_DIM, TILE), lambda i: (0, i)),
            pl.BlockSpec((1, TILE, N_KV_HEADS * HEAD_DIM), lambda i: (i, 0, 0)),
            pl.BlockSpec((N_KV_HEADS, 1, V_ROWS, TILE), vt_index),
        ],
        out_shape=[
            jax.ShapeDtypeStruct((N_Q_HEADS * HEAD_DIM, tp), BF16),
            jax.ShapeDtypeStruct((nt, TILE, N_KV_HEADS * HEAD_DIM), BF16),
            jax.ShapeDtypeStruct((N_KV_HEADS, batch * vchunks, V_ROWS, ATT_TK), BF16),
        ],
        compiler_params=_cparams(("arbitrary",)),
        name="attn_qkv",
    )(h, g.reshape(1, -1), w_t, q_gain.reshape(-1, 1), k_gain.reshape(-1, 1), cos_t, sin_t)


def _attn_body(qt_ref, k_ref, vt_ref, o_ref, qaug_ref, sa_ref, sb_ref, *, n_chunks, n_q, meta_rows):
    kvh = pl.program_id(1)
    tq = TILE
    tpc = ATT_TK // TILE
    meta_tile = n_chunks * tpc
    n_steps = n_q * n_chunks

    qaug_ref[...] = jnp.zeros(qaug_ref.shape, BF16)
    row0 = pl.multiple_of(kvh * HEAD_DIM, HEAD_DIM)
    for j in range(n_q):
        for g in range(GROUP):
            qaug_ref[j * GROUP + g, pl.ds(row0, HEAD_DIM), :] = (
                qt_ref[g * HEAD_DIM:(g + 1) * HEAD_DIM, j * tq:(j + 1) * tq])

    def scores(t, g, s_ref):
        qi = t // n_chunks
        c = t % n_chunks
        kc = k_ref[pl.ds(c * tpc, tpc)].reshape(ATT_TK, N_KV_HEADS * HEAD_DIM)
        s = jnp.dot(kc, qaug_ref[qi * GROUP + g], preferred_element_type=F32)
        s_ref[g] = s
        return jnp.max(s, axis=0, keepdims=True)

    def softmax_pv(c, g, s_ref, cmax, m, acc):
        m_new = jnp.maximum(m, cmax)
        alpha = jnp.exp2(m - m_new)
        p = jnp.exp2(s_ref[g] - m_new).astype(BF16)
        acc = alpha * acc + jnp.dot(vt_ref[0, c], p, preferred_element_type=F32)
        return m_new, acc

    def finalize(qi, state):
        k_meta = k_ref[meta_tile, 0:128, :]
        v_meta = vt_ref[0, n_chunks, :, 0:128]
        for g in range(GROUP):
            m, acc = state[g]
            s = jnp.dot(k_meta, qaug_ref[qi * GROUP + g], preferred_element_type=F32)
            rows = lax.broadcasted_iota(jnp.int32, s.shape, 0)
            s = jnp.where(rows < meta_rows, s, NEG_BIG)
            m_new = jnp.maximum(m, jnp.max(s, axis=0, keepdims=True))
            p = jnp.exp2(s - m_new).astype(BF16)
            acc = jnp.exp2(m - m_new) * acc + jnp.dot(v_meta, p, preferred_element_type=F32)
            out_t = acc[:HEAD_DIM] / acc[HEAD_DIM:HEAD_DIM + 1]
            o_ref[pl.ds(pl.multiple_of(qi * tq, tq), tq), g * HEAD_DIM:(g + 1) * HEAD_DIM] = (
                out_t.T.astype(BF16))

    def step(t_next, next_ref, cur_ref, cmaxes, state, reset):
        c_cur = (t_next - 1) % n_chunks
        new_cmax, new_state = [], []
        for g in range(GROUP):
            m, acc = state[g]
            if reset:
                first = c_cur == 0
                m = jnp.where(first, NEG_BIG, m)
                acc = jnp.where(first, 0.0, acc)
            new_cmax.append(scores(t_next, g, next_ref))
            new_state.append(softmax_pv(c_cur, g, cur_ref, cmaxes[g], m, acc))
        return tuple(new_cmax), tuple(new_state)

    bufs = (sa_ref, sb_ref)
    unroll = ATT_UNROLL
    assert unroll % 2 == 0 and n_chunks % unroll == 0

    def group(i, carry):
        cmaxes, state = carry
        t0 = unroll * i
        for u in range(1, unroll + 1):
            cmaxes, state = step(t0 + u, bufs[u % 2], bufs[(u - 1) % 2], cmaxes, state, reset=(u == 1))

        @pl.when((t0 + unroll - 1) % n_chunks == n_chunks - 1)
        def _():
            finalize((t0 + unroll - 1) // n_chunks, state)

        return cmaxes, state

    state = tuple((jnp.full((1, tq), NEG_BIG, F32), jnp.zeros((V_ROWS, tq), F32)) for _ in range(GROUP))
    cmaxes = tuple(scores(0, g, sa_ref) for g in range(GROUP))
    n_groups = n_steps // unroll - 1
    cmaxes, state = lax.fori_loop(0, n_groups, group, (cmaxes, state))
    for t in range(unroll * n_groups + 1, n_steps):
        cmaxes, state = step(t, bufs[t % 2], bufs[(t - 1) % 2], cmaxes, state, reset=True)
    state = tuple(softmax_pv(n_chunks - 1, g, bufs[(n_steps - 1) % 2], cmaxes[g], *state[g])
                  for g in range(GROUP))
    finalize(n_q - 1, state)


def _attention(q_t, k, v_t, batch, n_tok, lp):
    tp = q_t.shape[1]
    n_chunks = n_tok // ATT_TK
    tiles_per_batch = lp // TILE
    n_q = max(d for d in range(1, ATT_NQ + 1) if tiles_per_batch % d == 0)
    q_steps = tiles_per_batch // n_q
    vchunks = v_t.shape[1] // batch
    body = functools.partial(_attn_body, n_chunks=n_chunks, n_q=n_q, meta_rows=N_META)
    return pl.pallas_call(
        body,
        grid=(batch, N_KV_HEADS, q_steps),
        in_specs=[
            pl.BlockSpec((GROUP * HEAD_DIM, n_q * TILE), lambda b, h, i: (h, b * q_steps + i)),
            pl.BlockSpec((tiles_per_batch, TILE, N_KV_HEADS * HEAD_DIM), lambda b, h, i: (b, 0, 0)),
            pl.BlockSpec((1, vchunks, V_ROWS, ATT_TK), lambda b, h, i: (h, b, 0, 0)),
        ],
        out_specs=pl.BlockSpec((n_q * TILE, GROUP * HEAD_DIM), lambda b, h, i: (b * q_steps + i, h)),
        out_shape=jax.ShapeDtypeStruct((tp, N_Q_HEADS * HEAD_DIM), BF16),
        scratch_shapes=[pltpu.VMEM((n_q * GROUP, N_KV_HEADS * HEAD_DIM, TILE), BF16),
                        pltpu.VMEM((GROUP, ATT_TK, TILE), F32),
                        pltpu.VMEM((GROUP, ATT_TK, TILE), F32)],
        compiler_params=_cparams(("arbitrary", "arbitrary", "arbitrary")),
        name="attention",
    )(q_t, k, v_t)


def _proj_res_body(a_ref, h_ref, w_ref, g_ref, o_ref):
    m = jnp.dot(a_ref[...], w_ref[...], preferred_element_type=F32)
    o_ref[...] = h_ref[...] + _rms_rows(m, g_ref[...])


def _proj_residual(a, h, w, g):
    tp, kdim = a.shape
    return pl.pallas_call(
        _proj_res_body,
        grid=(tp // FFN_TM,),
        in_specs=[
            pl.BlockSpec((FFN_TM, kdim), lambda i: (i, 0)),
            pl.BlockSpec((FFN_TM, D_MODEL), lambda i: (i, 0)),
            _const_spec((kdim, D_MODEL)),
            _const_spec((1, D_MODEL)),
        ],
        out_specs=pl.BlockSpec((FFN_TM, D_MODEL), lambda i: (i, 0)),
        out_shape=jax.ShapeDtypeStruct(h.shape, F32),
        compiler_params=_cparams(("arbitrary",)),
        name="proj_residual",
    )(a, h, w, g.reshape(1, -1))


def _rec_in_body(h_ref, g_ref, w_ref, gate_ref, xr_ref):
    hn = _rms_rows(h_ref[...], g_ref[...]).astype(BF16)
    u = jnp.dot(hn, w_ref[...], preferred_element_type=F32)
    x = u[:, :D_RNN]
    cdf = 0.5 * (1.0 + jnp.tanh(0.7978845608028654 * (x + 0.044715 * (x * x * x))))
    gate_ref[...] = x * cdf
    xr_ref[...] = u[:, D_RNN:]


def _rec_in(h, g, w):
    tp = h.shape[0]
    spec = pl.BlockSpec((FFN_TM, D_MODEL), lambda i: (i, 0))
    return pl.pallas_call(
        _rec_in_body,
        grid=(tp // FFN_TM,),
        in_specs=[spec, _const_spec((1, D_MODEL)), _const_spec((D_MODEL, 2 * D_RNN))],
        out_specs=[spec, spec],
        out_shape=[jax.ShapeDtypeStruct((tp, D_RNN), F32)] * 2,
        compiler_params=_cparams(("arbitrary",)),
        name="rec_in",
    )(h, g.reshape(1, -1), w)


def _rglru_coeffs(x, prev8, next8, cw_ref, cb_ref, wr_ref, wi_ref, br_ref, bi_ref, lam_ref):
    tt = x.shape[0]
    ext = jnp.concatenate([prev8, x, next8], axis=0)
    n_ext = tt + 16
    xm1 = pltpu.roll(ext, 1, axis=0)[8:8 + tt]
    xm2 = pltpu.roll(ext, 2, axis=0)[8:8 + tt]
    xp1 = pltpu.roll(ext, n_ext - 1, axis=0)[8:8 + tt]
    cw = cw_ref[...]
    xc = cb_ref[...] + xm2 * cw[0:1] + xm1 * cw[1:2] + x * cw[2:3] + xp1 * cw[3:4]
    k = (-RG_C * LOG2E) * jnp.logaddexp(-lam_ref[...], 0.0)
    r_parts = []
    i_parts = []
    for blk in range(N_RG_BLOCKS):
        xb = xc[:, blk * RG_BW:(blk + 1) * RG_BW].astype(BF16)
        r_parts.append(jnp.dot(xb, wr_ref[blk], preferred_element_type=F32))
        i_parts.append(jnp.dot(xb, wi_ref[blk], preferred_element_type=F32))
    r = jax.nn.sigmoid(jnp.concatenate(r_parts, axis=1) + br_ref[...])
    i = jax.nn.sigmoid(jnp.concatenate(i_parts, axis=1) + bi_ref[...])
    a = jnp.exp2(r * k)
    b = jnp.sqrt(1.0 - a * a) * i * xc
    return a, b


def _store_coeffs(x_ref, prev_ref, next_ref, is_tail, has_prev, has_next, a_s, b_s, gate_refs):
    @pl.when(jnp.logical_not(is_tail))
    def _():
        prev8 = jnp.where(has_prev, prev_ref[...], 0.0)
        next8 = jnp.where(has_next, next_ref[...], 0.0)
        a, b = _rglru_coeffs(x_ref[...], prev8, next8, *gate_refs)
        a_s[...] = a
        b_s[...] = b

    @pl.when(is_tail)
    def _():
        x = x_ref[...]
        zeros8 = jnp.zeros((8, x.shape[1]), F32)
        x = jnp.concatenate([x[:N_META], next_ref[...], x[N_META + 8:]], axis=0)
        a, b = _rglru_coeffs(x, zeros8, zeros8, *gate_refs)
        valid = lax.broadcasted_iota(jnp.int32, x.shape, 0) < N_META
        a_s[...] = jnp.where(valid, a, 1.0)
        b_s[...] = jnp.where(valid, b, 0.0)


def _scan_fwd_body(x_ref, prev_ref, next_ref, cw_ref, cb_ref, wr_ref, wi_ref, br_ref, bi_ref,
                   lam_ref, hf_ref, a_s, b_s, carry_s, *, n_tiles):
    j = pl.program_id(1)
    _store_coeffs(x_ref, prev_ref, next_ref, is_tail=j == 0, has_prev=j > 0, has_next=j < n_tiles - 1,
                  a_s=a_s, b_s=b_s,
                  gate_refs=(cw_ref, cb_ref, wr_ref, wi_ref, br_ref, bi_ref, lam_ref))

    @pl.when(j == 0)
    def _():
        carry_s[...] = jnp.zeros(carry_s.shape, F32)

    def step(r, c):
        c = a_s[pl.ds(r, 1), :] * c + b_s[pl.ds(r, 1), :]
        hf_ref[pl.ds(r, 1), :] = c
        return c

    carry_s[...] = lax.fori_loop(0, TILE, step, carry_s[...], unroll=8)


def _scan_bwd_body(x_ref, prev_ref, next_ref, cw_ref, cb_ref, wr_ref, wi_ref, br_ref, bi_ref,
                   lam_ref, hf_ref, gate_ref, h_ref, wout_ref, g_ref, o_ref,
                   a_s, b_s, hb_s, carry_s, *, n_tiles):
    j = pl.program_id(1)
    _store_coeffs(x_ref, prev_ref, next_ref, is_tail=j == n_tiles - 1, has_prev=j < n_tiles - 1,
                  has_next=j > 0, a_s=a_s, b_s=b_s,
                  gate_refs=(cw_ref, cb_ref, wr_ref, wi_ref, br_ref, bi_ref, lam_ref))

    @pl.when(j == 0)
    def _():
        carry_s[...] = jnp.zeros(carry_s.shape, F32)

    def step(t, c):
        r = TILE - 1 - t
        c = a_s[pl.ds(r, 1), :] * c + b_s[pl.ds(r, 1), :]
        hb_s[pl.ds(r, 1), :] = c
        return c

    carry_s[...] = lax.fori_loop(0, TILE, step, carry_s[...], unroll=8)
    y = ((hf_ref[...] + hb_s[...]) * gate_ref[...]).astype(BF16)
    m = jnp.dot(y, wout_ref[...], preferred_element_type=F32)
    o_ref[...] = h_ref[...] + _rms_rows(m, g_ref[...])


def _scan_specs(batch, n_tiles, reverse):
    n_x = n_tiles - 1
    rpt = TILE // 8

    def logical(j):
        return (n_tiles - 1 - j) if reverse else j

    def phys_tile(b, j):
        lj = logical(j)
        return b * n_tiles + jnp.where(lj == 0, n_x, lj - 1)

    def prev_blk(b, j):
        lj = logical(j)
        base = b * n_tiles * rpt
        idx = jnp.where(lj <= 1, n_x * rpt + 1, (lj - 1) * rpt - 1)
        return base + idx

    def next_blk(b, j):
        lj = logical(j)
        base = b * n_tiles * rpt
        idx = jnp.where(lj == 0, 0, jnp.minimum(lj, n_x - 1) * rpt)
        return base + idx

    tile_spec = pl.BlockSpec((TILE, D_RNN), lambda b, j: (phys_tile(b, j), 0))
    prev_spec = pl.BlockSpec((8, D_RNN), lambda b, j: (prev_blk(b, j), 0))
    next_spec = pl.BlockSpec((8, D_RNN), lambda b, j: (next_blk(b, j), 0))
    return tile_spec, prev_spec, next_spec


def _gate_specs():
    return [
        _const_spec((4, D_RNN)),
        _const_spec((1, D_RNN)),
        _const_spec((N_RG_BLOCKS, RG_BW, RG_BW)),
        _const_spec((N_RG_BLOCKS, RG_BW, RG_BW)),
        _const_spec((1, D_RNN)),
        _const_spec((1, D_RNN)),
        _const_spec((1, D_RNN)),
    ]


def _scan_fwd(xr, batch, conv_w, conv_b, wr, wi, br, bi, lam):
    tp = xr.shape[0]
    n_tiles = tp // TILE // batch
    tile_spec, prev_spec, next_spec = _scan_specs(batch, n_tiles, reverse=False)
    return pl.pallas_call(
        functools.partial(_scan_fwd_body, n_tiles=n_tiles),
        grid=(batch, n_tiles),
        in_specs=[tile_spec, prev_spec, next_spec] + _gate_specs(),
        out_specs=tile_spec,
        out_shape=jax.ShapeDtypeStruct((tp, D_RNN), F32),
        scratch_shapes=[pltpu.VMEM((TILE, D_RNN), F32), pltpu.VMEM((TILE, D_RNN), F32),
                        pltpu.VMEM((1, D_RNN), F32)],
        compiler_params=_cparams(("arbitrary", "arbitrary")),
        name="scan_fwd",
    )(xr, xr, xr, conv_w, conv_b, wr, wi, br, bi, lam)


def _scan_bwd(xr, batch, conv_w, conv_b, wr, wi, br, bi, lam, hf, gate, h, w_out, g):
    tp = xr.shape[0]
    n_tiles = tp // TILE // batch
    tile_spec, prev_spec, next_spec = _scan_specs(batch, n_tiles, reverse=True)
    return pl.pallas_call(
        functools.partial(_scan_bwd_body, n_tiles=n_tiles),
        grid=(batch, n_tiles),
        in_specs=[tile_spec, prev_spec, next_spec] + _gate_specs()
        + [tile_spec, tile_spec, tile_spec, _const_spec((D_RNN, D_MODEL)), _const_spec((1, D_MODEL))],
        out_specs=tile_spec,
        out_shape=jax.ShapeDtypeStruct((tp, D_MODEL), F32),
        scratch_shapes=[pltpu.VMEM((TILE, D_RNN), F32), pltpu.VMEM((TILE, D_RNN), F32),
                        pltpu.VMEM((TILE, D_RNN), F32), pltpu.VMEM((1, D_RNN), F32)],
        compiler_params=_cparams(("arbitrary", "arbitrary")),
        name="scan_bwd",
    )(xr, xr, xr, conv_w, conv_b, wr, wi, br, bi, lam, hf, gate, h, w_out, g.reshape(1, -1))


def _rope_tables_t(n_tok, lp, batch):
    n = jnp.arange(n_tok, dtype=jnp.int32)
    row = (n // GRID_W).astype(F32)
    col = (n % GRID_W).astype(F32)
    zeros = jnp.zeros((lp - n_tok,), F32)
    row = jnp.concatenate([row, zeros])
    col = jnp.concatenate([col, zeros])
    inv = 1.0 / (ROPE_THETA ** (jnp.arange(0, ROPE_AXIS, 2, dtype=F32) / ROPE_AXIS))
    theta_t = jnp.concatenate([inv[:, None] * row[None, :], inv[:, None] * col[None, :]], axis=0)
    theta_t = jnp.tile(theta_t, (1, batch))
    return jnp.cos(theta_t), jnp.sin(theta_t)


def kernel(x, meta_tokens, norm_gains, ffn_w_in, ffn_w_out, attn_w_qkv, attn_q_gain, attn_k_gain,
           attn_w_o, rec_w_in, rec_conv_w, rec_conv_b, rec_gate_w, rec_gate_b, rec_lambda, rec_w_out):
    batch, n_tok, d = x.shape
    assert d == D_MODEL and n_tok % TILE == 0 and N_META % 8 == 0 and N_META + 8 <= TILE
    lp = n_tok + TILE
    assert (batch * lp) % FFN_TM == 0 and n_tok % (ATT_UNROLL * ATT_TK) == 0 and N_META <= 128
    depth = norm_gains.shape[0]

    meta = jnp.broadcast_to(meta_tokens.astype(x.dtype)[None], (batch, N_META, d))
    pad = jnp.zeros((batch, TILE - N_META, d), x.dtype)
    h = jnp.concatenate([x, meta, pad], axis=1).reshape(batch * lp, d)
    cos_t, sin_t = _rope_tables_t(n_tok, lp, batch)

    for layer in range(depth):
        g = norm_gains[layer]
        h = _ffn(h, g[0], g[1], ffn_w_in[layer, 0].astype(BF16), ffn_w_out[layer, 0].astype(BF16))
        j = layer // 2
        if layer % 2 == 0:
            q_t, k, v_t = _attn_qkv(h, g[2], attn_w_qkv[j].T.astype(BF16), attn_q_gain[j],
                                    attn_k_gain[j], cos_t, sin_t, batch)
            o = _attention(q_t, k, v_t, batch, n_tok, lp)
            h = _proj_residual(o, h, attn_w_o[j].astype(BF16), g[3])
        else:
            gate, xr = _rec_in(h, g[2], rec_w_in[j].astype(BF16))
            gw = rec_gate_w[j].astype(BF16)
            gb = rec_gate_b[j]
            lam = rec_lambda[j]
            cw = rec_conv_w[j]
            cb = rec_conv_b[j].reshape(1, -1)
            hf = _scan_fwd(xr, batch, cw, cb, gw[0, 0], gw[0, 1], gb[0, 0].reshape(1, -1),
                           gb[0, 1].reshape(1, -1), lam[0].reshape(1, -1))
            h = _scan_bwd(xr, batch, cw, cb, gw[1, 0], gw[1, 1], gb[1, 0].reshape(1, -1),
                          gb[1, 1].reshape(1, -1), lam[1].reshape(1, -1), hf, gate, h,
                          rec_w_out[j].astype(BF16), g[3])
        h = _ffn(h, g[4], g[5], ffn_w_in[layer, 1].astype(BF16), ffn_w_out[layer, 1].astype(BF16))
    return h.reshape(batch, lp, d)[:, :n_tok]
```

```python
import functools

import jax
import jax.numpy as jnp
from jax import lax
from jax.experimental import pallas as pl
from jax.experimental.pallas import tpu as pltpu

D_MODEL = 1024
N_META = 16
GRID_W = 64
HEAD_DIM = 64
N_Q_HEADS = 16
N_KV_HEADS = 4
GROUP = 4
QKV_WIDTH = 1536
ROPE_AXIS = 32
ROPE_THETA = 10000.0
D_RNN = 1024
N_RG_BLOCKS = 4
RG_BW = 256
RG_C = 8.0
D_FF = 2816
NORM_EPS = 1e-6

VMEM_LIMIT_BYTES = 56 * 1024 * 1024

TILE = 256
ATT_TK = 1024
ATT_NQ = 13
ATT_UNROLL = 8
LOG2E = 1.4426950408889634
FFN_TM = 1280
FFN_SUB = 256
V_ROWS = 80
NEG_BIG = -1e30

BF16 = jnp.bfloat16
F32 = jnp.float32


def _cparams(sem):
    return pltpu.CompilerParams(dimension_semantics=sem, vmem_limit_bytes=VMEM_LIMIT_BYTES)


def _rms_rows(x, g):
    return x * lax.rsqrt(jnp.mean(x * x, axis=-1, keepdims=True) + NORM_EPS) * g


def _row_block(tp):
    return max(m for m in range(FFN_SUB, FFN_TM + 1, FFN_SUB) if tp % m == 0)


def _const_spec(shape):
    zeros = (0,) * len(shape)
    return pl.BlockSpec(shape, lambda *_: zeros)


FF_CHUNKS = (768, 768, 768, 512)


def _ffn_rows(load_x, n_sub, sub, gpre, gpost, win_ref, wout_ref, o_ref):
    def prep(s):
        x = load_x(s)
        return x, _rms_rows(x, gpre).astype(BF16)

    def finish(s, x, acc):
        o_ref[s * sub:(s + 1) * sub, :] = x + 0.5 * _rms_rows(acc, gpost)

    cur = prep(0)
    prev = None
    for s in range(n_sub):
        x, xn = cur
        acc = None
        c0 = 0
        for ci, ck in enumerate(FF_CHUNKS):
            a = jnp.dot(xn, win_ref[:, c0:c0 + ck], preferred_element_type=F32)
            b = jnp.dot(xn, win_ref[:, D_FF + c0:D_FF + c0 + ck], preferred_element_type=F32)
            g = (a * jax.nn.sigmoid(a) * b).astype(BF16)
            d = jnp.dot(g, wout_ref[c0:c0 + ck, :], preferred_element_type=F32)
            acc = d if acc is None else acc + d
            c0 += ck
            if ci == 0:
                if s + 1 < n_sub:
                    cur = prep(s + 1)
                if prev is not None:
                    finish(*prev)
        prev = (s, x, acc)
    finish(*prev)


def _ffn_body(h_ref, gpre_ref, gpost_ref, win_ref, wout_ref, o_ref):
    def load_x(s):
        return h_ref[s * FFN_SUB:(s + 1) * FFN_SUB, :]

    _ffn_rows(load_x, h_ref.shape[0] // FFN_SUB, FFN_SUB, gpre_ref[...], gpost_ref[...], win_ref, wout_ref, o_ref)


def _proj_ffn_body(a_ref, h_ref, wp_ref, gp_ref, gpre_ref, gpost_ref, win_ref, wout_ref, o_ref):
    def load_x(s):
        rows = slice(s * FFN_SUB, (s + 1) * FFN_SUB)
        m = jnp.dot(a_ref[rows, :], wp_ref[...], preferred_element_type=F32)
        return h_ref[rows, :] + _rms_rows(m, gp_ref[...])

    _ffn_rows(load_x, h_ref.shape[0] // FFN_SUB, FFN_SUB, gpre_ref[...], gpost_ref[...], win_ref, wout_ref, o_ref)


def _ffn_weight_specs(layer, idx):
    return [
        pl.BlockSpec((None, None, D_MODEL, 2 * D_FF), lambda i: (layer, idx, 0, 0),
                     pipeline_mode=pl.Buffered(1)),
        pl.BlockSpec((None, None, D_FF, D_MODEL), lambda i: (layer, idx, 0, 0),
                     pipeline_mode=pl.Buffered(1)),
    ]


def _ffn(h, gpre, gpost, w_in, w_out, layer, idx):
    tp = h.shape[0]
    tm = _row_block(tp)
    row_spec = pl.BlockSpec((tm, D_MODEL), lambda i: (i, 0))
    return pl.pallas_call(
        _ffn_body,
        grid=(tp // tm,),
        in_specs=[row_spec, _const_spec((1, D_MODEL)), _const_spec((1, D_MODEL))]
        + _ffn_weight_specs(layer, idx),
        out_specs=row_spec,
        out_shape=jax.ShapeDtypeStruct(h.shape, F32),
        compiler_params=_cparams(("arbitrary",)),
        name="ffn",
    )(h, gpre.reshape(1, -1), gpost.reshape(1, -1), w_in, w_out)


def _proj_ffn(a, h, wp, gp, gpre, gpost, w_in, w_out, layer, idx):
    tp, kdim = a.shape
    tm = _row_block(tp)
    row_spec = pl.BlockSpec((tm, D_MODEL), lambda i: (i, 0))
    return pl.pallas_call(
        _proj_ffn_body,
        grid=(tp // tm,),
        in_specs=[pl.BlockSpec((tm, kdim), lambda i: (i, 0)), row_spec,
                  pl.BlockSpec((kdim, D_MODEL), lambda i: (0, 0), pipeline_mode=pl.Buffered(1)),
                  _const_spec((1, D_MODEL)), _const_spec((1, D_MODEL)), _const_spec((1, D_MODEL))]
        + _ffn_weight_specs(layer, idx),
        out_specs=row_spec,
        out_shape=jax.ShapeDtypeStruct(h.shape, F32),
        compiler_params=_cparams(("arbitrary",)),
        name="proj_ffn",
    )(a, h, wp, gp.reshape(1, -1), gpre.reshape(1, -1), gpost.reshape(1, -1), w_in, w_out)


def _head_norm_rope(blk, gain, cos, sin):
    y = blk * lax.rsqrt(jnp.mean(blk * blk, axis=0, keepdims=True) + NORM_EPS) * gain
    x1 = y[:ROPE_AXIS]
    x2 = y[ROPE_AXIS:]
    return jnp.concatenate([x1 * cos - x2 * sin, x1 * sin + x2 * cos], axis=0)


def _qkv_body(h_ref, g_ref, wt_ref, qg_ref, kg_ref, cos_ref, sin_ref, qt_ref, k_ref, vt_ref):
    hn = _rms_rows(h_ref[...], g_ref[...]).astype(BF16)
    qkv_t = lax.dot_general(wt_ref[...], hn, (((1,), (1,)), ((), ())),
                            preferred_element_type=F32)
    cos = cos_ref[...]
    sin = sin_ref[...]
    qg = qg_ref[...]
    kg = kg_ref[...]
    for hq in range(N_Q_HEADS):
        blk = qkv_t[hq * HEAD_DIM:(hq + 1) * HEAD_DIM]
        q = _head_norm_rope(blk, qg, cos, sin) * (HEAD_DIM ** -0.5 * LOG2E)
        qt_ref[hq * HEAD_DIM:(hq + 1) * HEAD_DIM, :] = q.astype(BF16)
    k_off = N_Q_HEADS * HEAD_DIM
    k_heads = []
    for hk in range(N_KV_HEADS):
        blk = qkv_t[k_off + hk * HEAD_DIM:k_off + (hk + 1) * HEAD_DIM]
        k_heads.append(_head_norm_rope(blk, kg, cos, sin))
    k_t = jnp.concatenate(k_heads, axis=0)
    k_ref[0] = k_t.T.astype(BF16)
    v_off = k_off + N_KV_HEADS * HEAD_DIM
    tm = hn.shape[0]
    ones_rows = (lax.broadcasted_iota(jnp.int32, (V_ROWS - HEAD_DIM, tm), 0) == 0).astype(BF16)
    for hk in range(N_KV_HEADS):
        v = qkv_t[v_off + hk * HEAD_DIM:v_off + (hk + 1) * HEAD_DIM].astype(BF16)
        vt_ref[hk, 0] = jnp.concatenate([v, ones_rows], axis=0)


def _attn_qkv(h, g, w_t, q_gain, k_gain, cos_t, sin_t):
    tp = h.shape[0]
    nt = tp // TILE
    return pl.pallas_call(
        _qkv_body,
        grid=(nt,),
        in_specs=[
            pl.BlockSpec((TILE, D_MODEL), lambda i: (i, 0)),
            _const_spec((1, D_MODEL)),
            _const_spec((QKV_WIDTH, D_MODEL)),
            _const_spec((HEAD_DIM, 1)),
            _const_spec((HEAD_DIM, 1)),
            pl.BlockSpec((ROPE_AXIS, TILE), lambda i: (0, i)),
            pl.BlockSpec((ROPE_AXIS, TILE), lambda i: (0, i)),
        ],
        out_specs=[
            pl.BlockSpec((N_Q_HEADS * HEAD_DIM, TILE), lambda i: (0, i)),
            pl.BlockSpec((1, TILE, N_KV_HEADS * HEAD_DIM), lambda i: (i, 0, 0)),
            pl.BlockSpec((N_KV_HEADS, 1, V_ROWS, TILE), lambda i: (0, i, 0, 0)),
        ],
        out_shape=[
            jax.ShapeDtypeStruct((N_Q_HEADS * HEAD_DIM, tp), BF16),
            jax.ShapeDtypeStruct((nt, TILE, N_KV_HEADS * HEAD_DIM), BF16),
            jax.ShapeDtypeStruct((N_KV_HEADS, nt, V_ROWS, TILE), BF16),
        ],
        compiler_params=_cparams(("arbitrary",)),
        name="attn_qkv",
    )(h, g.reshape(1, -1), w_t, q_gain.reshape(-1, 1), k_gain.reshape(-1, 1), cos_t, sin_t)


def _attn_body(qt_ref, k_ref, vt_ref, o_ref, qaug_ref, sa_ref, sb_ref, *, n_chunks, n_q, meta_rows):
    kvh = pl.program_id(1)
    tq = TILE
    tpc = ATT_TK // TILE
    meta_tile = n_chunks * tpc
    n_steps = n_q * n_chunks

    qaug_ref[...] = jnp.zeros(qaug_ref.shape, BF16)
    row0 = pl.multiple_of(kvh * HEAD_DIM, HEAD_DIM)
    for j in range(n_q):
        for g in range(GROUP):
            qaug_ref[j * GROUP + g, pl.ds(row0, HEAD_DIM), :] = (
                qt_ref[g * HEAD_DIM:(g + 1) * HEAD_DIM, j * tq:(j + 1) * tq])

    def scores(t, g, s_ref):
        qi = t // n_chunks
        c = t % n_chunks
        kc = k_ref[pl.ds(c * tpc, tpc)].reshape(ATT_TK, N_KV_HEADS * HEAD_DIM)
        s = jnp.dot(kc, qaug_ref[qi * GROUP + g], preferred_element_type=F32)
        s_ref[g] = s
        return jnp.max(s, axis=0, keepdims=True)

    def v_chunk(c):
        return jnp.concatenate([vt_ref[0, c * tpc + u] for u in range(tpc)], axis=1)

    def softmax_pv(vc, g, s_ref, cmax, m, acc):
        m_new = jnp.maximum(m, cmax)
        alpha = jnp.exp2(m - m_new)
        p = jnp.exp2(s_ref[g] - m_new).astype(BF16)
        acc = alpha * acc + jnp.dot(vc, p, preferred_element_type=F32)
        return m_new, acc

    def finalize(qi, state):
        k_meta = k_ref[meta_tile, 0:128, :]
        v_meta = vt_ref[0, meta_tile, :, 0:128]
        for g in range(GROUP):
            m, acc = state[g]
            s = jnp.dot(k_meta, qaug_ref[qi * GROUP + g], preferred_element_type=F32)
            rows = lax.broadcasted_iota(jnp.int32, s.shape, 0)
            s = jnp.where(rows < meta_rows, s, NEG_BIG)
            m_new = jnp.maximum(m, jnp.max(s, axis=0, keepdims=True))
            p = jnp.exp2(s - m_new).astype(BF16)
            acc = jnp.exp2(m - m_new) * acc + jnp.dot(v_meta, p, preferred_element_type=F32)
            out_t = acc[:HEAD_DIM] / acc[HEAD_DIM:HEAD_DIM + 1]
            o_ref[pl.ds(pl.multiple_of(qi * tq, tq), tq), g * HEAD_DIM:(g + 1) * HEAD_DIM] = (
                out_t.T.astype(BF16))

    def step(t_next, next_ref, cur_ref, cmaxes, state, reset):
        c_cur = (t_next - 1) % n_chunks
        vc = v_chunk(c_cur)
        new_cmax, new_state = [], []
        for g in range(GROUP):
            m, acc = state[g]
            if reset:
                first = c_cur == 0
                m = jnp.where(first, NEG_BIG, m)
                acc = jnp.where(first, 0.0, acc)
            new_cmax.append(scores(t_next, g, next_ref))
            new_state.append(softmax_pv(vc, g, cur_ref, cmaxes[g], m, acc))
        return tuple(new_cmax), tuple(new_state)

    bufs = (sa_ref, sb_ref)
    unroll = ATT_UNROLL
    assert unroll % 2 == 0 and n_chunks % unroll == 0

    def group(i, carry):
        cmaxes, state = carry
        t0 = unroll * i
        for u in range(1, unroll + 1):
            cmaxes, state = step(t0 + u, bufs[u % 2], bufs[(u - 1) % 2], cmaxes, state, reset=(u == 1))

        @pl.when((t0 + unroll - 1) % n_chunks == n_chunks - 1)
        def _():
            finalize((t0 + unroll - 1) // n_chunks, state)

        return cmaxes, state

    state = tuple((jnp.full((1, tq), NEG_BIG, F32), jnp.zeros((V_ROWS, tq), F32)) for _ in range(GROUP))
    cmaxes = tuple(scores(0, g, sa_ref) for g in range(GROUP))
    n_groups = n_steps // unroll - 1
    cmaxes, state = lax.fori_loop(0, n_groups, group, (cmaxes, state))
    for t in range(unroll * n_groups + 1, n_steps):
        cmaxes, state = step(t, bufs[t % 2], bufs[(t - 1) % 2], cmaxes, state, reset=True)
    vc_last = v_chunk(n_chunks - 1)
    state = tuple(softmax_pv(vc_last, g, bufs[(n_steps - 1) % 2], cmaxes[g], *state[g])
                  for g in range(GROUP))
    finalize(n_q - 1, state)


def _attention(q_t, k, v_t, batch, n_tok, lp):
    tp = q_t.shape[1]
    n_chunks = n_tok // ATT_TK
    tiles_per_batch = lp // TILE
    n_q = max(d for d in range(1, ATT_NQ + 1) if tiles_per_batch % d == 0)
    q_steps = tiles_per_batch // n_q
    body = functools.partial(_attn_body, n_chunks=n_chunks, n_q=n_q, meta_rows=N_META)
    return pl.pallas_call(
        body,
        grid=(batch, N_KV_HEADS, q_steps),
        in_specs=[
            pl.BlockSpec((GROUP * HEAD_DIM, n_q * TILE), lambda b, h, i: (h, b * q_steps + i)),
            pl.BlockSpec((tiles_per_batch, TILE, N_KV_HEADS * HEAD_DIM), lambda b, h, i: (b, 0, 0)),
            pl.BlockSpec((1, tiles_per_batch, V_ROWS, TILE), lambda b, h, i: (h, b, 0, 0)),
        ],
        out_specs=pl.BlockSpec((n_q * TILE, GROUP * HEAD_DIM), lambda b, h, i: (b * q_steps + i, h)),
        out_shape=jax.ShapeDtypeStruct((tp, N_Q_HEADS * HEAD_DIM), BF16),
        scratch_shapes=[pltpu.VMEM((n_q * GROUP, N_KV_HEADS * HEAD_DIM, TILE), BF16),
                        pltpu.VMEM((GROUP, ATT_TK, TILE), F32),
                        pltpu.VMEM((GROUP, ATT_TK, TILE), F32)],
        compiler_params=_cparams(("arbitrary", "arbitrary", "arbitrary")),
        name="attention",
    )(q_t, k, v_t)


def _rec_in_body(h_ref, g_ref, w_ref, gate_ref, xr_ref):
    hn = _rms_rows(h_ref[...], g_ref[...]).astype(BF16)
    u = jnp.dot(hn, w_ref[...], preferred_element_type=F32)
    x = u[:, :D_RNN]
    cdf = 0.5 * (1.0 + jnp.tanh(0.7978845608028654 * (x + 0.044715 * (x * x * x))))
    gate_ref[...] = x * cdf
    xr_ref[...] = u[:, D_RNN:]


def _rec_in(h, g, w):
    tp = h.shape[0]
    tm = _row_block(tp)
    spec = pl.BlockSpec((tm, D_MODEL), lambda i: (i, 0))
    return pl.pallas_call(
        _rec_in_body,
        grid=(tp // tm,),
        in_specs=[spec, _const_spec((1, D_MODEL)), _const_spec((D_MODEL, 2 * D_RNN))],
        out_specs=[spec, spec],
        out_shape=[jax.ShapeDtypeStruct((tp, D_RNN), F32)] * 2,
        compiler_params=_cparams(("arbitrary",)),
        name="rec_in",
    )(h, g.reshape(1, -1), w)


def _rglru_coeffs(x, prev8, next8, cw_ref, cb_ref, wr_ref, wi_ref, br_ref, bi_ref, lam_ref):
    tt = x.shape[0]
    ext = jnp.concatenate([prev8, x, next8], axis=0)
    n_ext = tt + 16
    xm1 = pltpu.roll(ext, 1, axis=0)[8:8 + tt]
    xm2 = pltpu.roll(ext, 2, axis=0)[8:8 + tt]
    xp1 = pltpu.roll(ext, n_ext - 1, axis=0)[8:8 + tt]
    cw = cw_ref[...]
    xc = cb_ref[...] + xm2 * cw[0:1] + xm1 * cw[1:2] + x * cw[2:3] + xp1 * cw[3:4]
    k = (-RG_C * LOG2E) * jnp.logaddexp(-lam_ref[...], 0.0)
    r_parts = []
    i_parts = []
    for blk in range(N_RG_BLOCKS):
        xb = xc[:, blk * RG_BW:(blk + 1) * RG_BW].astype(BF16)
        r_parts.append(jnp.dot(xb, wr_ref[blk], preferred_element_type=F32))
        i_parts.append(jnp.dot(xb, wi_ref[blk], preferred_element_type=F32))
    r = jax.nn.sigmoid(jnp.concatenate(r_parts, axis=1) + br_ref[...])
    i = jax.nn.sigmoid(jnp.concatenate(i_parts, axis=1) + bi_ref[...])
    a = jnp.exp2(r * k)
    b = jnp.sqrt(1.0 - a * a) * i * xc
    return a, b


def _store_coeffs(x_ref, prev_ref, next_ref, is_tail, has_prev, has_next, a_s, b_s, gate_refs):
    @pl.when(jnp.logical_not(is_tail))
    def _():
        prev8 = jnp.where(has_prev, prev_ref[...], 0.0)
        next8 = jnp.where(has_next, next_ref[...], 0.0)
        a, b = _rglru_coeffs(x_ref[...], prev8, next8, *gate_refs)
        a_s[...] = a
        b_s[...] = b

    @pl.when(is_tail)
    def _():
        x = x_ref[...]
        zeros8 = jnp.zeros((8, x.shape[1]), F32)
        x = jnp.concatenate([x[:N_META], next_ref[...], x[N_META + 8:]], axis=0)
        a, b = _rglru_coeffs(x, zeros8, zeros8, *gate_refs)
        valid = lax.broadcasted_iota(jnp.int32, x.shape, 0) < N_META
        a_s[...] = jnp.where(valid, a, 1.0)
        b_s[...] = jnp.where(valid, b, 0.0)


def _scan_fwd_body(x_ref, prev_ref, next_ref, cw_ref, cb_ref, wr_ref, wi_ref, br_ref, bi_ref,
                   lam_ref, hf_ref, a_s, b_s, carry_s, *, n_tiles):
    j = pl.program_id(1)
    _store_coeffs(x_ref, prev_ref, next_ref, is_tail=j == 0, has_prev=j > 0, has_next=j < n_tiles - 1,
                  a_s=a_s, b_s=b_s,
                  gate_refs=(cw_ref, cb_ref, wr_ref, wi_ref, br_ref, bi_ref, lam_ref))

    @pl.when(j == 0)
    def _():
        carry_s[...] = jnp.zeros(carry_s.shape, F32)

    def step(r, c):
        c = a_s[pl.ds(r, 1), :] * c + b_s[pl.ds(r, 1), :]
        hf_ref[pl.ds(r, 1), :] = c
        return c

    carry_s[...] = lax.fori_loop(0, TILE, step, carry_s[...], unroll=8)


def _scan_bwd_body(x_ref, prev_ref, next_ref, cw_ref, cb_ref, wr_ref, wi_ref, br_ref, bi_ref,
                   lam_ref, hf_ref, gate_ref, h_ref, wout_ref, g_ref, o_ref,
                   a_s, b_s, hb_s, carry_s, *, n_tiles):
    j = pl.program_id(1)
    _store_coeffs(x_ref, prev_ref, next_ref, is_tail=j == n_tiles - 1, has_prev=j < n_tiles - 1,
                  has_next=j > 0, a_s=a_s, b_s=b_s,
                  gate_refs=(cw_ref, cb_ref, wr_ref, wi_ref, br_ref, bi_ref, lam_ref))

    @pl.when(j == 0)
    def _():
        carry_s[...] = jnp.zeros(carry_s.shape, F32)

    def step(t, c):
        r = TILE - 1 - t
        c = a_s[pl.ds(r, 1), :] * c + b_s[pl.ds(r, 1), :]
        hb_s[pl.ds(r, 1), :] = c
        return c

    carry_s[...] = lax.fori_loop(0, TILE, step, carry_s[...], unroll=8)
    y = ((hf_ref[...] + hb_s[...]) * gate_ref[...]).astype(BF16)
    m = jnp.dot(y, wout_ref[...], preferred_element_type=F32)
    o_ref[...] = h_ref[...] + _rms_rows(m, g_ref[...])


def _scan_specs(batch, n_tiles, reverse):
    n_x = n_tiles - 1
    rpt = TILE // 8

    def logical(j):
        return (n_tiles - 1 - j) if reverse else j

    def phys_tile(b, j):
        lj = logical(j)
        return b * n_tiles + jnp.where(lj == 0, n_x, lj - 1)

    def prev_blk(b, j):
        lj = logical(j)
        base = b * n_tiles * rpt
        idx = jnp.where(lj <= 1, n_x * rpt + 1, (lj - 1) * rpt - 1)
        return base + idx

    def next_blk(b, j):
        lj = logical(j)
        base = b * n_tiles * rpt
        idx = jnp.where(lj == 0, 0, jnp.minimum(lj, n_x - 1) * rpt)
        return base + idx

    tile_spec = pl.BlockSpec((TILE, D_RNN), lambda b, j: (phys_tile(b, j), 0))
    prev_spec = pl.BlockSpec((8, D_RNN), lambda b, j: (prev_blk(b, j), 0))
    next_spec = pl.BlockSpec((8, D_RNN), lambda b, j: (next_blk(b, j), 0))
    return tile_spec, prev_spec, next_spec


def _gate_specs():
    return [
        _const_spec((4, D_RNN)),
        _const_spec((1, D_RNN)),
        _const_spec((N_RG_BLOCKS, RG_BW, RG_BW)),
        _const_spec((N_RG_BLOCKS, RG_BW, RG_BW)),
        _const_spec((1, D_RNN)),
        _const_spec((1, D_RNN)),
        _const_spec((1, D_RNN)),
    ]


def _scan_fwd(xr, batch, conv_w, conv_b, wr, wi, br, bi, lam):
    tp = xr.shape[0]
    n_tiles = tp // TILE // batch
    tile_spec, prev_spec, next_spec = _scan_specs(batch, n_tiles, reverse=False)
    return pl.pallas_call(
        functools.partial(_scan_fwd_body, n_tiles=n_tiles),
        grid=(batch, n_tiles),
        in_specs=[tile_spec, prev_spec, next_spec] + _gate_specs(),
        out_specs=tile_spec,
        out_shape=jax.ShapeDtypeStruct((tp, D_RNN), F32),
        scratch_shapes=[pltpu.VMEM((TILE, D_RNN), F32), pltpu.VMEM((TILE, D_RNN), F32),
                        pltpu.VMEM((1, D_RNN), F32)],
        compiler_params=_cparams(("arbitrary", "arbitrary")),
        name="scan_fwd",
    )(xr, xr, xr, conv_w, conv_b, wr, wi, br, bi, lam)


def _scan_bwd(xr, batch, conv_w, conv_b, wr, wi, br, bi, lam, hf, gate, h, w_out, g):
    tp = xr.shape[0]
    n_tiles = tp // TILE // batch
    tile_spec, prev_spec, next_spec = _scan_specs(batch, n_tiles, reverse=True)
    return pl.pallas_call(
        functools.partial(_scan_bwd_body, n_tiles=n_tiles),
        grid=(batch, n_tiles),
        in_specs=[tile_spec, prev_spec, next_spec] + _gate_specs()
        + [tile_spec, tile_spec, tile_spec, _const_spec((D_RNN, D_MODEL)), _const_spec((1, D_MODEL))],
        out_specs=tile_spec,
        out_shape=jax.ShapeDtypeStruct((tp, D_MODEL), F32),
        scratch_shapes=[pltpu.VMEM((TILE, D_RNN), F32), pltpu.VMEM((TILE, D_RNN), F32),
                        pltpu.VMEM((TILE, D_RNN), F32), pltpu.VMEM((1, D_RNN), F32)],
        compiler_params=_cparams(("arbitrary", "arbitrary")),
        name="scan_bwd",
    )(xr, xr, xr, conv_w, conv_b, wr, wi, br, bi, lam, hf, gate, h, w_out, g.reshape(1, -1))


def _rope_tables_t(n_tok, lp, batch):
    n = jnp.arange(n_tok, dtype=jnp.int32)
    row = (n // GRID_W).astype(F32)
    col = (n % GRID_W).astype(F32)
    zeros = jnp.zeros((lp - n_tok,), F32)
    row = jnp.concatenate([row, zeros])
    col = jnp.concatenate([col, zeros])
    inv = 1.0 / (ROPE_THETA ** (jnp.arange(0, ROPE_AXIS, 2, dtype=F32) / ROPE_AXIS))
    theta_t = jnp.concatenate([inv[:, None] * row[None, :], inv[:, None] * col[None, :]], axis=0)
    theta_t = jnp.tile(theta_t, (1, batch))
    return jnp.cos(theta_t), jnp.sin(theta_t)


def kernel(x, meta_tokens, norm_gains, ffn_w_in, ffn_w_out, attn_w_qkv, attn_q_gain, attn_k_gain,
           attn_w_o, rec_w_in, rec_conv_w, rec_conv_b, rec_gate_w, rec_gate_b, rec_lambda, rec_w_out):
    batch, n_tok, d = x.shape
    assert d == D_MODEL and n_tok % TILE == 0 and N_META % 8 == 0 and N_META + 8 <= TILE
    lp = n_tok + TILE
    assert TILE % FFN_SUB == 0 and n_tok % (ATT_UNROLL * ATT_TK) == 0 and N_META <= 128
    depth = norm_gains.shape[0]

    meta = jnp.broadcast_to(meta_tokens.astype(x.dtype)[None], (batch, N_META, d))
    pad = jnp.zeros((batch, TILE - N_META, d), x.dtype)
    h = jnp.concatenate([x, meta, pad], axis=1).reshape(batch * lp, d)
    cos_t, sin_t = _rope_tables_t(n_tok, lp, batch)

    w_in = ffn_w_in.astype(BF16)
    w_out = ffn_w_out.astype(BF16)
    for layer in range(depth):
        g = norm_gains[layer]
        h = _ffn(h, g[0], g[1], w_in, w_out, layer, 0)
        j = layer // 2
        if layer % 2 == 0:
            q_t, k, v_t = _attn_qkv(h, g[2], attn_w_qkv[j].T.astype(BF16), attn_q_gain[j],
                                    attn_k_gain[j], cos_t, sin_t)
            o = _attention(q_t, k, v_t, batch, n_tok, lp)
            h = _proj_ffn(o, h, attn_w_o[j].astype(BF16), g[3], g[4], g[5], w_in, w_out, layer, 1)
        else:
            gate, xr = _rec_in(h, g[2], rec_w_in[j].astype(BF16))
            gw = rec_gate_w[j].astype(BF16)
            gb = rec_gate_b[j]
            lam = rec_lambda[j]
            cw = rec_conv_w[j]
            cb = rec_conv_b[j].reshape(1, -1)
            hf = _scan_fwd(xr, batch, cw, cb, gw[0, 0], gw[0, 1], gb[0, 0].reshape(1, -1),
                           gb[0, 1].reshape(1, -1), lam[0].reshape(1, -1))
            h = _scan_bwd(xr, batch, cw, cb, gw[1, 0], gw[1, 1], gb[1, 0].reshape(1, -1),
                          gb[1, 1].reshape(1, -1), lam[1].reshape(1, -1), hf, gate, h,
                          rec_w_out[j].astype(BF16), g[3])
            h = _ffn(h, g[4], g[5], w_in, w_out, layer, 1)
    return h.reshape(batch, lp, d)[:, :n_tok]
```

```python
import functools

import jax
import jax.numpy as jnp
from jax import lax
from jax.experimental import pallas as pl
from jax.experimental.pallas import tpu as pltpu

D_MODEL = 1024
N_META = 16
GRID_W = 64
HEAD_DIM = 64
N_Q_HEADS = 16
N_KV_HEADS = 4
GROUP = 4
QKV_WIDTH = 1536
ROPE_AXIS = 32
ROPE_THETA = 10000.0
D_RNN = 1024
N_RG_BLOCKS = 4
RG_BW = 256
RG_C = 8.0
D_FF = 2816
NORM_EPS = 1e-6

VMEM_LIMIT_BYTES = 56 * 1024 * 1024

TILE = 256
ATT_TK = 1024
ATT_NQ = 13
ATT_UNROLL = 8
LOG2E = 1.4426950408889634
FFN_TM = 1280
FFN_SUB = 256
V_ROWS = 80
NEG_BIG = -1e30

BF16 = jnp.bfloat16
F32 = jnp.float32


def _cparams(sem):
    return pltpu.CompilerParams(dimension_semantics=sem, vmem_limit_bytes=VMEM_LIMIT_BYTES)


def _rms_rows(x, g):
    return x * lax.rsqrt(jnp.mean(x * x, axis=-1, keepdims=True) + NORM_EPS) * g


def _row_block(tp):
    return max(m for m in range(FFN_SUB, FFN_TM + 1, FFN_SUB) if tp % m == 0)


def _const_spec(shape):
    zeros = (0,) * len(shape)
    return pl.BlockSpec(shape, lambda *_: zeros)


FF_CHUNKS = (768, 768, 768, 512)


def _ffn_rows(load_x, n_sub, sub, gpre, gpost, win_ref, wout_ref, o_ref):
    def prep(s):
        x = load_x(s)
        return x, _rms_rows(x, gpre).astype(BF16)

    def finish(s, x, acc):
        o_ref[s * sub:(s + 1) * sub, :] = x + 0.5 * _rms_rows(acc, gpost)

    cur = prep(0)
    prev = None
    for s in range(n_sub):
        x, xn = cur
        acc = None
        c0 = 0
        for ci, ck in enumerate(FF_CHUNKS):
            a = jnp.dot(xn, win_ref[:, c0:c0 + ck], preferred_element_type=F32)
            b = jnp.dot(xn, win_ref[:, D_FF + c0:D_FF + c0 + ck], preferred_element_type=F32)
            g = (a * jax.nn.sigmoid(a) * b).astype(BF16)
            d = jnp.dot(g, wout_ref[c0:c0 + ck, :], preferred_element_type=F32)
            acc = d if acc is None else acc + d
            c0 += ck
            if ci == 0:
                if s + 1 < n_sub:
                    cur = prep(s + 1)
                if prev is not None:
                    finish(*prev)
        prev = (s, x, acc)
    finish(*prev)


def _ffn_body(h_ref, gpre_ref, gpost_ref, win_ref, wout_ref, o_ref):
    def load_x(s):
        return h_ref[s * FFN_SUB:(s + 1) * FFN_SUB, :]

    _ffn_rows(load_x, h_ref.shape[0] // FFN_SUB, FFN_SUB, gpre_ref[...], gpost_ref[...], win_ref, wout_ref, o_ref)


def _proj_ffn_body(a_ref, h_ref, wp_ref, gp_ref, gpre_ref, gpost_ref, win_ref, wout_ref, o_ref):
    def load_x(s):
        rows = slice(s * FFN_SUB, (s + 1) * FFN_SUB)
        m = jnp.dot(a_ref[rows, :], wp_ref[...], preferred_element_type=F32)
        return h_ref[rows, :] + _rms_rows(m, gp_ref[...])

    _ffn_rows(load_x, h_ref.shape[0] // FFN_SUB, FFN_SUB, gpre_ref[...], gpost_ref[...], win_ref, wout_ref, o_ref)


def _ffn_weight_specs(layer, idx):
    return [
        pl.BlockSpec((None, None, D_MODEL, 2 * D_FF), lambda i: (layer, idx, 0, 0),
                     pipeline_mode=pl.Buffered(1)),
        pl.BlockSpec((None, None, D_FF, D_MODEL), lambda i: (layer, idx, 0, 0),
                     pipeline_mode=pl.Buffered(1)),
    ]


def _ffn(h, gpre, gpost, w_in, w_out, layer, idx):
    tp = h.shape[0]
    tm = _row_block(tp)
    row_spec = pl.BlockSpec((tm, D_MODEL), lambda i: (i, 0))
    return pl.pallas_call(
        _ffn_body,
        grid=(tp // tm,),
        in_specs=[row_spec, _const_spec((1, D_MODEL)), _const_spec((1, D_MODEL))]
        + _ffn_weight_specs(layer, idx),
        out_specs=row_spec,
        out_shape=jax.ShapeDtypeStruct(h.shape, F32),
        compiler_params=_cparams(("arbitrary",)),
        name="ffn",
    )(h, gpre.reshape(1, -1), gpost.reshape(1, -1), w_in, w_out)


def _proj_ffn(a, h, wp, gp, gpre, gpost, w_in, w_out, layer, idx):
    tp, kdim = a.shape
    tm = _row_block(tp)
    row_spec = pl.BlockSpec((tm, D_MODEL), lambda i: (i, 0))
    return pl.pallas_call(
        _proj_ffn_body,
        grid=(tp // tm,),
        in_specs=[pl.BlockSpec((tm, kdim), lambda i: (i, 0)), row_spec,
                  pl.BlockSpec((kdim, D_MODEL), lambda i: (0, 0), pipeline_mode=pl.Buffered(1)),
                  _const_spec((1, D_MODEL)), _const_spec((1, D_MODEL)), _const_spec((1, D_MODEL))]
        + _ffn_weight_specs(layer, idx),
        out_specs=row_spec,
        out_shape=jax.ShapeDtypeStruct(h.shape, F32),
        compiler_params=_cparams(("arbitrary",)),
        name="proj_ffn",
    )(a, h, wp, gp.reshape(1, -1), gpre.reshape(1, -1), gpost.reshape(1, -1), w_in, w_out)


def _head_norm_rope(blk, gain, cos, sin):
    y = blk * lax.rsqrt(jnp.mean(blk * blk, axis=0, keepdims=True) + NORM_EPS) * gain
    x1 = y[:ROPE_AXIS]
    x2 = y[ROPE_AXIS:]
    return jnp.concatenate([x1 * cos - x2 * sin, x1 * sin + x2 * cos], axis=0)


def _qkv_body(h_ref, g_ref, wt_ref, qg_ref, kg_ref, cos_ref, sin_ref, qt_ref, k_ref, vt_ref):
    n_sub = k_ref.shape[0]
    qg = qg_ref[...]
    kg = kg_ref[...]
    ones_rows = (lax.broadcasted_iota(jnp.int32, (V_ROWS - HEAD_DIM, TILE), 0) == 0).astype(BF16)

    def project(s):
        hn = _rms_rows(h_ref[s * TILE:(s + 1) * TILE, :], g_ref[...]).astype(BF16)
        return lax.dot_general(wt_ref[...], hn, (((1,), (1,)), ((), ())),
                               preferred_element_type=F32)

    def emit(s, qkv_t):
        cols = slice(s * TILE, (s + 1) * TILE)
        cos = cos_ref[:, cols]
        sin = sin_ref[:, cols]
        for hq in range(N_Q_HEADS):
            blk = qkv_t[hq * HEAD_DIM:(hq + 1) * HEAD_DIM]
            q = _head_norm_rope(blk, qg, cos, sin) * (HEAD_DIM ** -0.5 * LOG2E)
            qt_ref[hq * HEAD_DIM:(hq + 1) * HEAD_DIM, cols] = q.astype(BF16)
        k_off = N_Q_HEADS * HEAD_DIM
        k_heads = []
        for hk in range(N_KV_HEADS):
            blk = qkv_t[k_off + hk * HEAD_DIM:k_off + (hk + 1) * HEAD_DIM]
            k_heads.append(_head_norm_rope(blk, kg, cos, sin))
        k_t = jnp.concatenate(k_heads, axis=0)
        k_ref[s] = k_t.T.astype(BF16)
        v_off = k_off + N_KV_HEADS * HEAD_DIM
        for hk in range(N_KV_HEADS):
            v = qkv_t[v_off + hk * HEAD_DIM:v_off + (hk + 1) * HEAD_DIM].astype(BF16)
            vt_ref[hk, s] = jnp.concatenate([v, ones_rows], axis=0)

    cur = project(0)
    for s in range(n_sub):
        nxt = project(s + 1) if s + 1 < n_sub else None
        emit(s, cur)
        cur = nxt


def _attn_qkv(h, g, w_t, q_gain, k_gain, cos_t, sin_t):
    tp = h.shape[0]
    nt = tp // TILE
    tm = _row_block(tp)
    n_sub = tm // TILE
    return pl.pallas_call(
        _qkv_body,
        grid=(tp // tm,),
        in_specs=[
            pl.BlockSpec((tm, D_MODEL), lambda i: (i, 0)),
            _const_spec((1, D_MODEL)),
            _const_spec((QKV_WIDTH, D_MODEL)),
            _const_spec((HEAD_DIM, 1)),
            _const_spec((HEAD_DIM, 1)),
            pl.BlockSpec((ROPE_AXIS, tm), lambda i: (0, i)),
            pl.BlockSpec((ROPE_AXIS, tm), lambda i: (0, i)),
        ],
        out_specs=[
            pl.BlockSpec((N_Q_HEADS * HEAD_DIM, tm), lambda i: (0, i)),
            pl.BlockSpec((n_sub, TILE, N_KV_HEADS * HEAD_DIM), lambda i: (i, 0, 0)),
            pl.BlockSpec((N_KV_HEADS, n_sub, V_ROWS, TILE), lambda i: (0, i, 0, 0)),
        ],
        out_shape=[
            jax.ShapeDtypeStruct((N_Q_HEADS * HEAD_DIM, tp), BF16),
            jax.ShapeDtypeStruct((nt, TILE, N_KV_HEADS * HEAD_DIM), BF16),
            jax.ShapeDtypeStruct((N_KV_HEADS, nt, V_ROWS, TILE), BF16),
        ],
        compiler_params=_cparams(("arbitrary",)),
        name="attn_qkv",
    )(h, g.reshape(1, -1), w_t, q_gain.reshape(-1, 1), k_gain.reshape(-1, 1), cos_t, sin_t)


def _attn_body(qt_ref, k_ref, vt_ref, o_ref, qaug_ref, vc_ref, sa_ref, sb_ref, *, n_chunks, n_q, meta_rows):
    kvh = pl.program_id(1)
    tq = TILE
    tpc = ATT_TK // TILE
    meta_tile = n_chunks * tpc
    n_steps = n_q * n_chunks

    qaug_ref[...] = jnp.zeros(qaug_ref.shape, BF16)
    row0 = pl.multiple_of(kvh * HEAD_DIM, HEAD_DIM)
    for j in range(n_q):
        for g in range(GROUP):
            qaug_ref[j * GROUP + g, pl.ds(row0, HEAD_DIM), :] = (
                qt_ref[g * HEAD_DIM:(g + 1) * HEAD_DIM, j * tq:(j + 1) * tq])

    def scores(t, g, s_ref):
        qi = t // n_chunks
        c = t % n_chunks
        kc = k_ref[pl.ds(c * tpc, tpc)].reshape(ATT_TK, N_KV_HEADS * HEAD_DIM)
        s = jnp.dot(kc, qaug_ref[qi * GROUP + g], preferred_element_type=F32)
        s_ref[g] = s
        return jnp.max(s, axis=0, keepdims=True)

    @pl.when(pl.program_id(2) == 0)
    def _():
        for c in range(n_chunks):
            for u in range(tpc):
                vc_ref[c, :, u * TILE:(u + 1) * TILE] = vt_ref[0, c * tpc + u]

    def v_chunk(c):
        return vc_ref[c]

    def softmax_pv(vc, g, s_ref, cmax, m, acc):
        m_new = jnp.maximum(m, cmax)
        alpha = jnp.exp2(m - m_new)
        p = jnp.exp2(s_ref[g] - m_new).astype(BF16)
        acc = alpha * acc + jnp.dot(vc, p, preferred_element_type=F32)
        return m_new, acc

    def finalize(qi, state):
        k_meta = k_ref[meta_tile, 0:128, :]
        v_meta = vt_ref[0, meta_tile, :, 0:128]
        for g in range(GROUP):
            m, acc = state[g]
            s = jnp.dot(k_meta, qaug_ref[qi * GROUP + g], preferred_element_type=F32)
            rows = lax.broadcasted_iota(jnp.int32, s.shape, 0)
            s = jnp.where(rows < meta_rows, s, NEG_BIG)
            m_new = jnp.maximum(m, jnp.max(s, axis=0, keepdims=True))
            p = jnp.exp2(s - m_new).astype(BF16)
            acc = jnp.exp2(m - m_new) * acc + jnp.dot(v_meta, p, preferred_element_type=F32)
            out_t = acc[:HEAD_DIM] / acc[HEAD_DIM:HEAD_DIM + 1]
            o_ref[pl.ds(pl.multiple_of(qi * tq, tq), tq), g * HEAD_DIM:(g + 1) * HEAD_DIM] = (
                out_t.T.astype(BF16))

    def step(t_next, next_ref, cur_ref, cmaxes, state, reset):
        c_cur = (t_next - 1) % n_chunks
        vc = v_chunk(c_cur)
        new_cmax, new_state = [], []
        for g in range(GROUP):
            m, acc = state[g]
            if reset:
                first = c_cur == 0
                m = jnp.where(first, NEG_BIG, m)
                acc = jnp.where(first, 0.0, acc)
            new_cmax.append(scores(t_next, g, next_ref))
            new_state.append(softmax_pv(vc, g, cur_ref, cmaxes[g], m, acc))
        return tuple(new_cmax), tuple(new_state)

    bufs = (sa_ref, sb_ref)
    unroll = ATT_UNROLL
    assert unroll % 2 == 0 and n_chunks % unroll == 0

    def group(i, carry):
        cmaxes, state = carry
        t0 = unroll * i
        for u in range(1, unroll + 1):
            cmaxes, state = step(t0 + u, bufs[u % 2], bufs[(u - 1) % 2], cmaxes, state, reset=(u == 1))

        @pl.when((t0 + unroll - 1) % n_chunks == n_chunks - 1)
        def _():
            finalize((t0 + unroll - 1) // n_chunks, state)

        return cmaxes, state

    state = tuple((jnp.full((1, tq), NEG_BIG, F32), jnp.zeros((V_ROWS, tq), F32)) for _ in range(GROUP))
    cmaxes = tuple(scores(0, g, sa_ref) for g in range(GROUP))
    n_groups = n_steps // unroll - 1
    cmaxes, state = lax.fori_loop(0, n_groups, group, (cmaxes, state))
    for t in range(unroll * n_groups + 1, n_steps):
        cmaxes, state = step(t, bufs[t % 2], bufs[(t - 1) % 2], cmaxes, state, reset=True)
    vc_last = v_chunk(n_chunks - 1)
    state = tuple(softmax_pv(vc_last, g, bufs[(n_steps - 1) % 2], cmaxes[g], *state[g])
                  for g in range(GROUP))
    finalize(n_q - 1, state)


def _attention(q_t, k, v_t, batch, n_tok, lp):
    tp = q_t.shape[1]
    n_chunks = n_tok // ATT_TK
    tiles_per_batch = lp // TILE
    n_q = max(d for d in range(1, ATT_NQ + 1) if tiles_per_batch % d == 0)
    q_steps = tiles_per_batch // n_q
    body = functools.partial(_attn_body, n_chunks=n_chunks, n_q=n_q, meta_rows=N_META)
    return pl.pallas_call(
        body,
        grid=(batch, N_KV_HEADS, q_steps),
        in_specs=[
            pl.BlockSpec((GROUP * HEAD_DIM, n_q * TILE), lambda b, h, i: (h, b * q_steps + i)),
            pl.BlockSpec((tiles_per_batch, TILE, N_KV_HEADS * HEAD_DIM), lambda b, h, i: (b, 0, 0)),
            pl.BlockSpec((1, tiles_per_batch, V_ROWS, TILE), lambda b, h, i: (h, b, 0, 0)),
        ],
        out_specs=pl.BlockSpec((n_q * TILE, GROUP * HEAD_DIM), lambda b, h, i: (b * q_steps + i, h)),
        out_shape=jax.ShapeDtypeStruct((tp, N_Q_HEADS * HEAD_DIM), BF16),
        scratch_shapes=[pltpu.VMEM((n_q * GROUP, N_KV_HEADS * HEAD_DIM, TILE), BF16),
                        pltpu.VMEM((n_chunks, V_ROWS, ATT_TK), BF16),
                        pltpu.VMEM((GROUP, ATT_TK, TILE), F32),
                        pltpu.VMEM((GROUP, ATT_TK, TILE), F32)],
        compiler_params=_cparams(("arbitrary", "arbitrary", "arbitrary")),
        name="attention",
    )(q_t, k, v_t)


def _rec_in_body(h_ref, g_ref, w_ref, gate_ref, xr_ref):
    hn = _rms_rows(h_ref[...], g_ref[...]).astype(BF16)
    u = jnp.dot(hn, w_ref[...], preferred_element_type=F32)
    x = u[:, :D_RNN]
    cdf = 0.5 * (1.0 + jnp.tanh(0.7978845608028654 * (x + 0.044715 * (x * x * x))))
    gate_ref[...] = x * cdf
    xr_ref[...] = u[:, D_RNN:]


def _rec_in(h, g, w):
    tp = h.shape[0]
    tm = _row_block(tp)
    spec = pl.BlockSpec((tm, D_MODEL), lambda i: (i, 0))
    return pl.pallas_call(
        _rec_in_body,
        grid=(tp // tm,),
        in_specs=[spec, _const_spec((1, D_MODEL)), _const_spec((D_MODEL, 2 * D_RNN))],
        out_specs=[spec, spec],
        out_shape=[jax.ShapeDtypeStruct((tp, D_RNN), F32)] * 2,
        compiler_params=_cparams(("arbitrary",)),
        name="rec_in",
    )(h, g.reshape(1, -1), w)


def _rglru_coeffs(x, prev8, next8, cw_ref, cb_ref, wr_ref, wi_ref, br_ref, bi_ref, lam_ref):
    tt = x.shape[0]
    ext = jnp.concatenate([prev8, x, next8], axis=0)
    n_ext = tt + 16
    xm1 = pltpu.roll(ext, 1, axis=0)[8:8 + tt]
    xm2 = pltpu.roll(ext, 2, axis=0)[8:8 + tt]
    xp1 = pltpu.roll(ext, n_ext - 1, axis=0)[8:8 + tt]
    cw = cw_ref[...]
    xc = cb_ref[...] + xm2 * cw[0:1] + xm1 * cw[1:2] + x * cw[2:3] + xp1 * cw[3:4]
    k = (-RG_C * LOG2E) * jnp.logaddexp(-lam_ref[...], 0.0)
    r_parts = []
    i_parts = []
    for blk in range(N_RG_BLOCKS):
        xb = xc[:, blk * RG_BW:(blk + 1) * RG_BW].astype(BF16)
        r_parts.append(jnp.dot(xb, wr_ref[blk], preferred_element_type=F32))
        i_parts.append(jnp.dot(xb, wi_ref[blk], preferred_element_type=F32))
    r = jax.nn.sigmoid(jnp.concatenate(r_parts, axis=1) + br_ref[...])
    i = jax.nn.sigmoid(jnp.concatenate(i_parts, axis=1) + bi_ref[...])
    a = jnp.exp2(r * k)
    b = jnp.sqrt(1.0 - a * a) * i * xc
    return a, b


def _store_coeffs(x_ref, prev_ref, next_ref, is_tail, has_prev, has_next, a_s, b_s, gate_refs):
    @pl.when(jnp.logical_not(is_tail))
    def _():
        prev8 = jnp.where(has_prev, prev_ref[...], 0.0)
        next8 = jnp.where(has_next, next_ref[...], 0.0)
        a, b = _rglru_coeffs(x_ref[...], prev8, next8, *gate_refs)
        a_s[...] = a
        b_s[...] = b

    @pl.when(is_tail)
    def _():
        x = x_ref[...]
        zeros8 = jnp.zeros((8, x.shape[1]), F32)
        x = jnp.concatenate([x[:N_META], next_ref[...], x[N_META + 8:]], axis=0)
        a, b = _rglru_coeffs(x, zeros8, zeros8, *gate_refs)
        valid = lax.broadcasted_iota(jnp.int32, x.shape, 0) < N_META
        a_s[...] = jnp.where(valid, a, 1.0)
        b_s[...] = jnp.where(valid, b, 0.0)


def _scan_fwd_body(x_ref, prev_ref, next_ref, cw_ref, cb_ref, wr_ref, wi_ref, br_ref, bi_ref,
                   lam_ref, hf_ref, a_s, b_s, carry_s, *, n_tiles):
    j = pl.program_id(1)
    _store_coeffs(x_ref, prev_ref, next_ref, is_tail=j == 0, has_prev=j > 0, has_next=j < n_tiles - 1,
                  a_s=a_s, b_s=b_s,
                  gate_refs=(cw_ref, cb_ref, wr_ref, wi_ref, br_ref, bi_ref, lam_ref))

    @pl.when(j == 0)
    def _():
        carry_s[...] = jnp.zeros(carry_s.shape, F32)

    def step(r, c):
        c = a_s[pl.ds(r, 1), :] * c + b_s[pl.ds(r, 1), :]
        hf_ref[pl.ds(r, 1), :] = c
        return c

    carry_s[...] = lax.fori_loop(0, TILE, step, carry_s[...], unroll=8)


def _scan_bwd_body(x_ref, prev_ref, next_ref, cw_ref, cb_ref, wr_ref, wi_ref, br_ref, bi_ref,
                   lam_ref, hf_ref, gate_ref, h_ref, wout_ref, g_ref, o_ref,
                   a_s, b_s, hb_s, carry_s, *, n_tiles):
    j = pl.program_id(1)
    _store_coeffs(x_ref, prev_ref, next_ref, is_tail=j == n_tiles - 1, has_prev=j < n_tiles - 1,
                  has_next=j > 0, a_s=a_s, b_s=b_s,
                  gate_refs=(cw_ref, cb_ref, wr_ref, wi_ref, br_ref, bi_ref, lam_ref))

    @pl.when(j == 0)
    def _():
        carry_s[...] = jnp.zeros(carry_s.shape, F32)

    def step(t, c):
        r = TILE - 1 - t
        c = a_s[pl.ds(r, 1), :] * c + b_s[pl.ds(r, 1), :]
        hb_s[pl.ds(r, 1), :] = c
        return c

    carry_s[...] = lax.fori_loop(0, TILE, step, carry_s[...], unroll=8)
    y = ((hf_ref[...] + hb_s[...]) * gate_ref[...]).astype(BF16)
    m = jnp.dot(y, wout_ref[...], preferred_element_type=F32)
    o_ref[...] = h_ref[...] + _rms_rows(m, g_ref[...])


def _scan_specs(batch, n_tiles, reverse):
    n_x = n_tiles - 1
    rpt = TILE // 8

    def logical(j):
        return (n_tiles - 1 - j) if reverse else j

    def phys_tile(b, j):
        lj = logical(j)
        return b * n_tiles + jnp.where(lj == 0, n_x, lj - 1)

    def prev_blk(b, j):
        lj = logical(j)
        base = b * n_tiles * rpt
        idx = jnp.where(lj <= 1, n_x * rpt + 1, (lj - 1) * rpt - 1)
        return base + idx

    def next_blk(b, j):
        lj = logical(j)
        base = b * n_tiles * rpt
        idx = jnp.where(lj == 0, 0, jnp.minimum(lj, n_x - 1) * rpt)
        return base + idx

    tile_spec = pl.BlockSpec((TILE, D_RNN), lambda b, j: (phys_tile(b, j), 0))
    prev_spec = pl.BlockSpec((8, D_RNN), lambda b, j: (prev_blk(b, j), 0))
    next_spec = pl.BlockSpec((8, D_RNN), lambda b, j: (next_blk(b, j), 0))
    return tile_spec, prev_spec, next_spec


def _gate_specs():
    return [
        _const_spec((4, D_RNN)),
        _const_spec((1, D_RNN)),
        _const_spec((N_RG_BLOCKS, RG_BW, RG_BW)),
        _const_spec((N_RG_BLOCKS, RG_BW, RG_BW)),
        _const_spec((1, D_RNN)),
        _const_spec((1, D_RNN)),
        _const_spec((1, D_RNN)),
    ]


def _scan_fwd(xr, batch, conv_w, conv_b, wr, wi, br, bi, lam):
    tp = xr.shape[0]
    n_tiles = tp // TILE // batch
    tile_spec, prev_spec, next_spec = _scan_specs(batch, n_tiles, reverse=False)
    return pl.pallas_call(
        functools.partial(_scan_fwd_body, n_tiles=n_tiles),
        grid=(batch, n_tiles),
        in_specs=[tile_spec, prev_spec, next_spec] + _gate_specs(),
        out_specs=tile_spec,
        out_shape=jax.ShapeDtypeStruct((tp, D_RNN), F32),
        scratch_shapes=[pltpu.VMEM((TILE, D_RNN), F32), pltpu.VMEM((TILE, D_RNN), F32),
                        pltpu.VMEM((1, D_RNN), F32)],
        compiler_params=_cparams(("arbitrary", "arbitrary")),
        name="scan_fwd",
    )(xr, xr, xr, conv_w, conv_b, wr, wi, br, bi, lam)


def _scan_bwd(xr, batch, conv_w, conv_b, wr, wi, br, bi, lam, hf, gate, h, w_out, g):
    tp = xr.shape[0]
    n_tiles = tp // TILE // batch
    tile_spec, prev_spec, next_spec = _scan_specs(batch, n_tiles, reverse=True)
    return pl.pallas_call(
        functools.partial(_scan_bwd_body, n_tiles=n_tiles),
        grid=(batch, n_tiles),
        in_specs=[tile_spec, prev_spec, next_spec] + _gate_specs()
        + [tile_spec, tile_spec, tile_spec, _const_spec((D_RNN, D_MODEL)), _const_spec((1, D_MODEL))],
        out_specs=tile_spec,
        out_shape=jax.ShapeDtypeStruct((tp, D_MODEL), F32),
        scratch_shapes=[pltpu.VMEM((TILE, D_RNN), F32), pltpu.VMEM((TILE, D_RNN), F32),
                        pltpu.VMEM((TILE, D_RNN), F32), pltpu.VMEM((1, D_RNN), F32)],
        compiler_params=_cparams(("arbitrary", "arbitrary")),
        name="scan_bwd",
    )(xr, xr, xr, conv_w, conv_b, wr, wi, br, bi, lam, hf, gate, h, w_out, g.reshape(1, -1))


def _rope_tables_t(n_tok, lp, batch):
    n = jnp.arange(n_tok, dtype=jnp.int32)
    row = (n // GRID_W).astype(F32)
    col = (n % GRID_W).astype(F32)
    zeros = jnp.zeros((lp - n_tok,), F32)
    row = jnp.concatenate([row, zeros])
    col = jnp.concatenate([col, zeros])
    inv = 1.0 / (ROPE_THETA ** (jnp.arange(0, ROPE_AXIS, 2, dtype=F32) / ROPE_AXIS))
    theta_t = jnp.concatenate([inv[:, None] * row[None, :], inv[:, None] * col[None, :]], axis=0)
    theta_t = jnp.tile(theta_t, (1, batch))
    return jnp.cos(theta_t), jnp.sin(theta_t)


def kernel(x, meta_tokens, norm_gains, ffn_w_in, ffn_w_out, attn_w_qkv, attn_q_gain, attn_k_gain,
           attn_w_o, rec_w_in, rec_conv_w, rec_conv_b, rec_gate_w, rec_gate_b, rec_lambda, rec_w_out):
    batch, n_tok, d = x.shape
    assert d == D_MODEL and n_tok % TILE == 0 and N_META % 8 == 0 and N_META + 8 <= TILE
    lp = n_tok + TILE
    assert TILE % FFN_SUB == 0 and n_tok % (ATT_UNROLL * ATT_TK) == 0 and N_META <= 128
    depth = norm_gains.shape[0]

    meta = jnp.broadcast_to(meta_tokens.astype(x.dtype)[None], (batch, N_META, d))
    pad = jnp.zeros((batch, TILE - N_META, d), x.dtype)
    h = jnp.concatenate([x, meta, pad], axis=1).reshape(batch * lp, d)
    cos_t, sin_t = _rope_tables_t(n_tok, lp, batch)

    w_in = ffn_w_in.astype(BF16)
    w_out = ffn_w_out.astype(BF16)
    for layer in range(depth):
        g = norm_gains[layer]
        h = _ffn(h, g[0], g[1], w_in, w_out, layer, 0)
        j = layer // 2
        if layer % 2 == 0:
            q_t, k, v_t = _attn_qkv(h, g[2], attn_w_qkv[j].T.astype(BF16), attn_q_gain[j],
                                    attn_k_gain[j], cos_t, sin_t)
            o = _attention(q_t, k, v_t, batch, n_tok, lp)
            h = _proj_ffn(o, h, attn_w_o[j].astype(BF16), g[3], g[4], g[5], w_in, w_out, layer, 1)
        else:
            gate, xr = _rec_in(h, g[2], rec_w_in[j].astype(BF16))
            gw = rec_gate_w[j].astype(BF16)
            gb = rec_gate_b[j]
            lam = rec_lambda[j]
            cw = rec_conv_w[j]
            cb = rec_conv_b[j].reshape(1, -1)
            hf = _scan_fwd(xr, batch, cw, cb, gw[0, 0], gw[0, 1], gb[0, 0].reshape(1, -1),
                           gb[0, 1].reshape(1, -1), lam[0].reshape(1, -1))
            h = _scan_bwd(xr, batch, cw, cb, gw[1, 0], gw[1, 1], gb[1, 0].reshape(1, -1),
                          gb[1, 1].reshape(1, -1), lam[1].reshape(1, -1), hf, gate, h,
                          rec_w_out[j].astype(BF16), g[3])
            h = _ffn(h, g[4], g[5], w_in, w_out, layer, 1)
    return h.reshape(batch, lp, d)[:, :n_tok]
```

```python
import functools

import jax
import jax.numpy as jnp
from jax import lax
from jax.experimental import pallas as pl
from jax.experimental.pallas import tpu as pltpu

D_MODEL = 1024
N_META = 16
GRID_W = 64
HEAD_DIM = 64
N_Q_HEADS = 16
N_KV_HEADS = 4
GROUP = 4
QKV_WIDTH = 1536
ROPE_AXIS = 32
ROPE_THETA = 10000.0
D_RNN = 1024
N_RG_BLOCKS = 4
RG_BW = 256
RG_C = 8.0
D_FF = 2816
NORM_EPS = 1e-6

VMEM_LIMIT_BYTES = 56 * 1024 * 1024

TILE = 256
ATT_TK = 1024
ATT_NQ = 13
ATT_UNROLL = 8
LOG2E = 1.4426950408889634
FFN_TM = 1280
FFN_SUB = 256
META_KEY_ROWS = 128
V_ROWS = 80
NEG_BIG = -1e30

BF16 = jnp.bfloat16
F32 = jnp.float32


def _cparams(sem):
    return pltpu.CompilerParams(dimension_semantics=sem, vmem_limit_bytes=VMEM_LIMIT_BYTES)


def _rms_rows(x, g):
    return x * lax.rsqrt(jnp.mean(x * x, axis=-1, keepdims=True) + NORM_EPS) * g


def _row_block(tp):
    return max(m for m in range(FFN_SUB, FFN_TM + 1, FFN_SUB) if tp % m == 0)


def _const_spec(shape):
    zeros = (0,) * len(shape)
    return pl.BlockSpec(shape, lambda *_: zeros)


FF_CHUNKS = (768, 768, 768, 512)


def _ffn_rows(load_x, n_sub, sub, gpre, gpost, win_ref, wout_ref, o_ref):
    def prep(s):
        x = load_x(s)
        return x, _rms_rows(x, gpre).astype(BF16)

    def finish(s, x, acc):
        o_ref[s * sub:(s + 1) * sub, :] = x + 0.5 * _rms_rows(acc, gpost)

    cur = prep(0)
    prev = None
    for s in range(n_sub):
        x, xn = cur
        acc = None
        c0 = 0
        for ci, ck in enumerate(FF_CHUNKS):
            a = jnp.dot(xn, win_ref[:, c0:c0 + ck], preferred_element_type=F32)
            b = jnp.dot(xn, win_ref[:, D_FF + c0:D_FF + c0 + ck], preferred_element_type=F32)
            g = (a * jax.nn.sigmoid(a) * b).astype(BF16)
            d = jnp.dot(g, wout_ref[c0:c0 + ck, :], preferred_element_type=F32)
            acc = d if acc is None else acc + d
            c0 += ck
            if ci == 0:
                if s + 1 < n_sub:
                    cur = prep(s + 1)
                if prev is not None:
                    finish(*prev)
        prev = (s, x, acc)
    finish(*prev)


def _ffn_body(h_ref, gpre_ref, gpost_ref, win_ref, wout_ref, o_ref):
    def load_x(s):
        return h_ref[s * FFN_SUB:(s + 1) * FFN_SUB, :]

    _ffn_rows(load_x, h_ref.shape[0] // FFN_SUB, FFN_SUB, gpre_ref[...], gpost_ref[...], win_ref, wout_ref, o_ref)


def _proj_ffn_body(a_ref, h_ref, wp_ref, gp_ref, gpre_ref, gpost_ref, win_ref, wout_ref, o_ref):
    def load_x(s):
        rows = slice(s * FFN_SUB, (s + 1) * FFN_SUB)
        m = jnp.dot(a_ref[rows, :], wp_ref[...], preferred_element_type=F32)
        return h_ref[rows, :] + _rms_rows(m, gp_ref[...])

    _ffn_rows(load_x, h_ref.shape[0] // FFN_SUB, FFN_SUB, gpre_ref[...], gpost_ref[...], win_ref, wout_ref, o_ref)


def _ffn_weight_specs(layer, idx):
    return [
        pl.BlockSpec((None, None, D_MODEL, 2 * D_FF), lambda i: (layer, idx, 0, 0),
                     pipeline_mode=pl.Buffered(1)),
        pl.BlockSpec((None, None, D_FF, D_MODEL), lambda i: (layer, idx, 0, 0),
                     pipeline_mode=pl.Buffered(1)),
    ]


def _ffn(h, gpre, gpost, w_in, w_out, layer, idx):
    tp = h.shape[0]
    tm = _row_block(tp)
    row_spec = pl.BlockSpec((tm, D_MODEL), lambda i: (i, 0))
    return pl.pallas_call(
        _ffn_body,
        grid=(tp // tm,),
        in_specs=[row_spec, _const_spec((1, D_MODEL)), _const_spec((1, D_MODEL))]
        + _ffn_weight_specs(layer, idx),
        out_specs=row_spec,
        out_shape=jax.ShapeDtypeStruct(h.shape, F32),
        compiler_params=_cparams(("arbitrary",)),
        name="ffn",
    )(h, gpre.reshape(1, -1), gpost.reshape(1, -1), w_in, w_out)


def _proj_ffn(a, h, wp, gp, gpre, gpost, w_in, w_out, layer, idx):
    tp, kdim = a.shape
    tm = _row_block(tp)
    row_spec = pl.BlockSpec((tm, D_MODEL), lambda i: (i, 0))
    return pl.pallas_call(
        _proj_ffn_body,
        grid=(tp // tm,),
        in_specs=[pl.BlockSpec((tm, kdim), lambda i: (i, 0)), row_spec,
                  pl.BlockSpec((kdim, D_MODEL), lambda i: (0, 0), pipeline_mode=pl.Buffered(1)),
                  _const_spec((1, D_MODEL)), _const_spec((1, D_MODEL)), _const_spec((1, D_MODEL))]
        + _ffn_weight_specs(layer, idx),
        out_specs=row_spec,
        out_shape=jax.ShapeDtypeStruct(h.shape, F32),
        compiler_params=_cparams(("arbitrary",)),
        name="proj_ffn",
    )(a, h, wp, gp.reshape(1, -1), gpre.reshape(1, -1), gpost.reshape(1, -1), w_in, w_out)


def _head_norm_rope(blk, gain, cos, sin):
    y = blk * lax.rsqrt(jnp.mean(blk * blk, axis=0, keepdims=True) + NORM_EPS) * gain
    x1 = y[:ROPE_AXIS]
    x2 = y[ROPE_AXIS:]
    return jnp.concatenate([x1 * cos - x2 * sin, x1 * sin + x2 * cos], axis=0)


def _qkv_body(h_ref, g_ref, wt_ref, qg_ref, kg_ref, cos_ref, sin_ref, qt_ref, k_ref, vt_ref):
    n_sub = k_ref.shape[0]
    qg = qg_ref[...]
    kg = kg_ref[...]
    ones_rows = (lax.broadcasted_iota(jnp.int32, (V_ROWS - HEAD_DIM, TILE), 0) == 0).astype(BF16)

    def project(s):
        hn = _rms_rows(h_ref[s * TILE:(s + 1) * TILE, :], g_ref[...]).astype(BF16)
        return lax.dot_general(wt_ref[...], hn, (((1,), (1,)), ((), ())),
                               preferred_element_type=F32)

    def emit(s, qkv_t):
        cols = slice(s * TILE, (s + 1) * TILE)
        cos = cos_ref[:, cols]
        sin = sin_ref[:, cols]
        for hq in range(N_Q_HEADS):
            blk = qkv_t[hq * HEAD_DIM:(hq + 1) * HEAD_DIM]
            q = _head_norm_rope(blk, qg, cos, sin) * (HEAD_DIM ** -0.5 * LOG2E)
            qt_ref[hq * HEAD_DIM:(hq + 1) * HEAD_DIM, cols] = q.astype(BF16)
        k_off = N_Q_HEADS * HEAD_DIM
        k_heads = []
        for hk in range(N_KV_HEADS):
            blk = qkv_t[k_off + hk * HEAD_DIM:k_off + (hk + 1) * HEAD_DIM]
            k_heads.append(_head_norm_rope(blk, kg, cos, sin))
        k_t = jnp.concatenate(k_heads, axis=0)
        k_ref[s] = k_t.T.astype(BF16)
        v_off = k_off + N_KV_HEADS * HEAD_DIM
        for hk in range(N_KV_HEADS):
            v = qkv_t[v_off + hk * HEAD_DIM:v_off + (hk + 1) * HEAD_DIM].astype(BF16)
            vt_ref[hk, s] = jnp.concatenate([v, ones_rows], axis=0)

    cur = project(0)
    for s in range(n_sub):
        nxt = project(s + 1) if s + 1 < n_sub else None
        emit(s, cur)
        cur = nxt


def _attn_qkv(h, g, w_t, q_gain, k_gain, cos_t, sin_t):
    tp = h.shape[0]
    nt = tp // TILE
    tm = _row_block(tp)
    n_sub = tm // TILE
    return pl.pallas_call(
        _qkv_body,
        grid=(tp // tm,),
        in_specs=[
            pl.BlockSpec((tm, D_MODEL), lambda i: (i, 0)),
            _const_spec((1, D_MODEL)),
            _const_spec((QKV_WIDTH, D_MODEL)),
            _const_spec((HEAD_DIM, 1)),
            _const_spec((HEAD_DIM, 1)),
            pl.BlockSpec((ROPE_AXIS, tm), lambda i: (0, i)),
            pl.BlockSpec((ROPE_AXIS, tm), lambda i: (0, i)),
        ],
        out_specs=[
            pl.BlockSpec((N_Q_HEADS * HEAD_DIM, tm), lambda i: (0, i)),
            pl.BlockSpec((n_sub, TILE, N_KV_HEADS * HEAD_DIM), lambda i: (i, 0, 0)),
            pl.BlockSpec((N_KV_HEADS, n_sub, V_ROWS, TILE), lambda i: (0, i, 0, 0)),
        ],
        out_shape=[
            jax.ShapeDtypeStruct((N_Q_HEADS * HEAD_DIM, tp), BF16),
            jax.ShapeDtypeStruct((nt, TILE, N_KV_HEADS * HEAD_DIM), BF16),
            jax.ShapeDtypeStruct((N_KV_HEADS, nt, V_ROWS, TILE), BF16),
        ],
        compiler_params=_cparams(("arbitrary",)),
        name="attn_qkv",
    )(h, g.reshape(1, -1), w_t, q_gain.reshape(-1, 1), k_gain.reshape(-1, 1), cos_t, sin_t)


def _attn_body(qt_ref, k_ref, vt_ref, o_ref, qaug_ref, vc_ref, sa_ref, sb_ref, *, n_chunks, n_q, meta_rows):
    kvh = pl.program_id(1)
    tq = TILE
    tpc = ATT_TK // TILE
    meta_tile = n_chunks * tpc
    n_steps = n_q * n_chunks

    qaug_ref[...] = jnp.zeros(qaug_ref.shape, BF16)
    row0 = pl.multiple_of(kvh * HEAD_DIM, HEAD_DIM)
    for j in range(n_q):
        for g in range(GROUP):
            qaug_ref[j * GROUP + g, pl.ds(row0, HEAD_DIM), :] = (
                qt_ref[g * HEAD_DIM:(g + 1) * HEAD_DIM, j * tq:(j + 1) * tq])

    def scores(t, g, s_ref):
        qi = t // n_chunks
        c = t % n_chunks
        kc = k_ref[pl.ds(c * tpc, tpc)].reshape(ATT_TK, N_KV_HEADS * HEAD_DIM)
        s = jnp.dot(kc, qaug_ref[qi * GROUP + g], preferred_element_type=F32)
        s_ref[g] = s
        return jnp.max(s, axis=0, keepdims=True)

    @pl.when(pl.program_id(2) == 0)
    def _():
        for c in range(n_chunks):
            for u in range(tpc):
                vc_ref[c, :, u * TILE:(u + 1) * TILE] = vt_ref[0, c * tpc + u]

    def v_chunk(c):
        return vc_ref[c]

    def softmax_pv(vc, g, s_ref, cmax, m, acc):
        m_new = jnp.maximum(m, cmax)
        alpha = jnp.exp2(m - m_new)
        p = jnp.exp2(s_ref[g] - m_new).astype(BF16)
        acc = alpha * acc + jnp.dot(vc, p, preferred_element_type=F32)
        return m_new, acc

    def finalize(qi, state):
        k_meta = k_ref[meta_tile, 0:META_KEY_ROWS, :]
        v_meta = vt_ref[0, meta_tile, :, 0:META_KEY_ROWS]
        for g in range(GROUP):
            m, acc = state[g]
            s = jnp.dot(k_meta, qaug_ref[qi * GROUP + g], preferred_element_type=F32)
            rows = lax.broadcasted_iota(jnp.int32, s.shape, 0)
            s = jnp.where(rows < meta_rows, s, NEG_BIG)
            m_new = jnp.maximum(m, jnp.max(s, axis=0, keepdims=True))
            p = jnp.exp2(s - m_new).astype(BF16)
            acc = jnp.exp2(m - m_new) * acc + jnp.dot(v_meta, p, preferred_element_type=F32)
            out_t = acc[:HEAD_DIM] / acc[HEAD_DIM:HEAD_DIM + 1]
            o_ref[pl.ds(pl.multiple_of(qi * tq, tq), tq), g * HEAD_DIM:(g + 1) * HEAD_DIM] = (
                out_t.T.astype(BF16))

    def step(t_next, next_ref, cur_ref, cmaxes, state, reset):
        c_cur = (t_next - 1) % n_chunks
        vc = v_chunk(c_cur)
        new_cmax, new_state = [], []
        for g in range(GROUP):
            m, acc = state[g]
            if reset:
                first = c_cur == 0
                m = jnp.where(first, NEG_BIG, m)
                acc = jnp.where(first, 0.0, acc)
            new_cmax.append(scores(t_next, g, next_ref))
            new_state.append(softmax_pv(vc, g, cur_ref, cmaxes[g], m, acc))
        return tuple(new_cmax), tuple(new_state)

    bufs = (sa_ref, sb_ref)
    unroll = ATT_UNROLL
    assert unroll % 2 == 0 and n_chunks % unroll == 0

    def group(i, carry):
        cmaxes, state = carry
        t0 = unroll * i
        for u in range(1, unroll + 1):
            cmaxes, state = step(t0 + u, bufs[u % 2], bufs[(u - 1) % 2], cmaxes, state, reset=(u == 1))

        @pl.when((t0 + unroll - 1) % n_chunks == n_chunks - 1)
        def _():
            finalize((t0 + unroll - 1) // n_chunks, state)

        return cmaxes, state

    state = tuple((jnp.full((1, tq), NEG_BIG, F32), jnp.zeros((V_ROWS, tq), F32)) for _ in range(GROUP))
    cmaxes = tuple(scores(0, g, sa_ref) for g in range(GROUP))
    n_groups = n_steps // unroll - 1
    cmaxes, state = lax.fori_loop(0, n_groups, group, (cmaxes, state))
    for t in range(unroll * n_groups + 1, n_steps):
        cmaxes, state = step(t, bufs[t % 2], bufs[(t - 1) % 2], cmaxes, state, reset=True)
    vc_last = v_chunk(n_chunks - 1)
    state = tuple(softmax_pv(vc_last, g, bufs[(n_steps - 1) % 2], cmaxes[g], *state[g])
                  for g in range(GROUP))
    finalize(n_q - 1, state)


def _attention(q_t, k, v_t, batch, n_tok, lp):
    tp = q_t.shape[1]
    n_chunks = n_tok // ATT_TK
    tiles_per_batch = lp // TILE
    n_q = max(d for d in range(1, ATT_NQ + 1) if tiles_per_batch % d == 0)
    q_steps = tiles_per_batch // n_q
    body = functools.partial(_attn_body, n_chunks=n_chunks, n_q=n_q, meta_rows=N_META)
    return pl.pallas_call(
        body,
        grid=(batch, N_KV_HEADS, q_steps),
        in_specs=[
            pl.BlockSpec((GROUP * HEAD_DIM, n_q * TILE), lambda b, h, i: (h, b * q_steps + i)),
            pl.BlockSpec((tiles_per_batch, TILE, N_KV_HEADS * HEAD_DIM), lambda b, h, i: (b, 0, 0)),
            pl.BlockSpec((1, tiles_per_batch, V_ROWS, TILE), lambda b, h, i: (h, b, 0, 0)),
        ],
        out_specs=pl.BlockSpec((n_q * TILE, GROUP * HEAD_DIM), lambda b, h, i: (b * q_steps + i, h)),
        out_shape=jax.ShapeDtypeStruct((tp, N_Q_HEADS * HEAD_DIM), BF16),
        scratch_shapes=[pltpu.VMEM((n_q * GROUP, N_KV_HEADS * HEAD_DIM, TILE), BF16),
                        pltpu.VMEM((n_chunks, V_ROWS, ATT_TK), BF16),
                        pltpu.VMEM((GROUP, ATT_TK, TILE), F32),
                        pltpu.VMEM((GROUP, ATT_TK, TILE), F32)],
        compiler_params=_cparams(("arbitrary", "arbitrary", "arbitrary")),
        name="attention",
    )(q_t, k, v_t)


def _rec_in_body(h_ref, g_ref, w_ref, gate_ref, xr_ref):
    n_sub = h_ref.shape[0] // FFN_SUB

    def project(s):
        hn = _rms_rows(h_ref[s * FFN_SUB:(s + 1) * FFN_SUB, :], g_ref[...]).astype(BF16)
        return jnp.dot(hn, w_ref[...], preferred_element_type=F32)

    def emit(s, u):
        rows = slice(s * FFN_SUB, (s + 1) * FFN_SUB)
        x = u[:, :D_RNN]
        cdf = 0.5 * (1.0 + jnp.tanh(0.7978845608028654 * (x + 0.044715 * (x * x * x))))
        gate_ref[rows, :] = x * cdf
        xr_ref[rows, :] = u[:, D_RNN:]

    cur = project(0)
    for s in range(n_sub):
        nxt = project(s + 1) if s + 1 < n_sub else None
        emit(s, cur)
        cur = nxt


def _rec_in(h, g, w):
    tp = h.shape[0]
    tm = _row_block(tp)
    spec = pl.BlockSpec((tm, D_MODEL), lambda i: (i, 0))
    return pl.pallas_call(
        _rec_in_body,
        grid=(tp // tm,),
        in_specs=[spec, _const_spec((1, D_MODEL)), _const_spec((D_MODEL, 2 * D_RNN))],
        out_specs=[spec, spec],
        out_shape=[jax.ShapeDtypeStruct((tp, D_RNN), F32)] * 2,
        compiler_params=_cparams(("arbitrary",)),
        name="rec_in",
    )(h, g.reshape(1, -1), w)


def _rglru_coeffs(x, prev8, next8, cw_ref, cb_ref, wr_ref, wi_ref, br_ref, bi_ref, lam_ref):
    tt = x.shape[0]
    ext = jnp.concatenate([prev8, x, next8], axis=0)
    n_ext = tt + 16
    xm1 = pltpu.roll(ext, 1, axis=0)[8:8 + tt]
    xm2 = pltpu.roll(ext, 2, axis=0)[8:8 + tt]
    xp1 = pltpu.roll(ext, n_ext - 1, axis=0)[8:8 + tt]
    cw = cw_ref[...]
    xc = cb_ref[...] + xm2 * cw[0:1] + xm1 * cw[1:2] + x * cw[2:3] + xp1 * cw[3:4]
    k = (-RG_C * LOG2E) * jnp.logaddexp(-lam_ref[...], 0.0)
    r_parts = []
    i_parts = []
    for blk in range(N_RG_BLOCKS):
        xb = xc[:, blk * RG_BW:(blk + 1) * RG_BW].astype(BF16)
        r_parts.append(jnp.dot(xb, wr_ref[blk], preferred_element_type=F32))
        i_parts.append(jnp.dot(xb, wi_ref[blk], preferred_element_type=F32))
    r = jax.nn.sigmoid(jnp.concatenate(r_parts, axis=1) + br_ref[...])
    i = jax.nn.sigmoid(jnp.concatenate(i_parts, axis=1) + bi_ref[...])
    a = jnp.exp2(r * k)
    b = jnp.sqrt(1.0 - a * a) * i * xc
    return a, b


def _store_coeffs(x_ref, prev_ref, next_ref, is_tail, has_prev, has_next, a_s, b_s, gate_refs):
    @pl.when(jnp.logical_not(is_tail))
    def _():
        prev8 = jnp.where(has_prev, prev_ref[...], 0.0)
        next8 = jnp.where(has_next, next_ref[...], 0.0)
        a, b = _rglru_coeffs(x_ref[...], prev8, next8, *gate_refs)
        a_s[...] = a
        b_s[...] = b

    @pl.when(is_tail)
    def _():
        x = x_ref[...]
        zeros8 = jnp.zeros((8, x.shape[1]), F32)
        x = jnp.concatenate([x[:N_META], next_ref[...], x[N_META + 8:]], axis=0)
        a, b = _rglru_coeffs(x, zeros8, zeros8, *gate_refs)
        valid = lax.broadcasted_iota(jnp.int32, x.shape, 0) < N_META
        a_s[...] = jnp.where(valid, a, 1.0)
        b_s[...] = jnp.where(valid, b, 0.0)


def _scan_fwd_body(x_ref, prev_ref, next_ref, cw_ref, cb_ref, wr_ref, wi_ref, br_ref, bi_ref,
                   lam_ref, hf_ref, a_s, b_s, carry_s, *, n_tiles):
    j = pl.program_id(1)
    _store_coeffs(x_ref, prev_ref, next_ref, is_tail=j == 0, has_prev=j > 0, has_next=j < n_tiles - 1,
                  a_s=a_s, b_s=b_s,
                  gate_refs=(cw_ref, cb_ref, wr_ref, wi_ref, br_ref, bi_ref, lam_ref))

    @pl.when(j == 0)
    def _():
        carry_s[...] = jnp.zeros(carry_s.shape, F32)

    def step(r, c):
        c = a_s[pl.ds(r, 1), :] * c + b_s[pl.ds(r, 1), :]
        hf_ref[pl.ds(r, 1), :] = c
        return c

    carry_s[...] = lax.fori_loop(0, TILE, step, carry_s[...], unroll=8)


def _scan_bwd_body(x_ref, prev_ref, next_ref, cw_ref, cb_ref, wr_ref, wi_ref, br_ref, bi_ref,
                   lam_ref, hf_ref, gate_ref, h_ref, wout_ref, g_ref, o_ref,
                   a_s, b_s, hb_s, carry_s, *, n_tiles):
    j = pl.program_id(1)
    _store_coeffs(x_ref, prev_ref, next_ref, is_tail=j == n_tiles - 1, has_prev=j < n_tiles - 1,
                  has_next=j > 0, a_s=a_s, b_s=b_s,
                  gate_refs=(cw_ref, cb_ref, wr_ref, wi_ref, br_ref, bi_ref, lam_ref))

    @pl.when(j == 0)
    def _():
        carry_s[...] = jnp.zeros(carry_s.shape, F32)

    def step(t, c):
        r = TILE - 1 - t
        c = a_s[pl.ds(r, 1), :] * c + b_s[pl.ds(r, 1), :]
        hb_s[pl.ds(r, 1), :] = c
        return c

    carry_s[...] = lax.fori_loop(0, TILE, step, carry_s[...], unroll=8)
    y = ((hf_ref[...] + hb_s[...]) * gate_ref[...]).astype(BF16)
    m = jnp.dot(y, wout_ref[...], preferred_element_type=F32)
    o_ref[...] = h_ref[...] + _rms_rows(m, g_ref[...])


def _scan_specs(batch, n_tiles, reverse):
    n_x = n_tiles - 1
    rpt = TILE // 8

    def logical(j):
        return (n_tiles - 1 - j) if reverse else j

    def phys_tile(b, j):
        lj = logical(j)
        return b * n_tiles + jnp.where(lj == 0, n_x, lj - 1)

    def prev_blk(b, j):
        lj = logical(j)
        base = b * n_tiles * rpt
        idx = jnp.where(lj <= 1, n_x * rpt + 1, (lj - 1) * rpt - 1)
        return base + idx

    def next_blk(b, j):
        lj = logical(j)
        base = b * n_tiles * rpt
        idx = jnp.where(lj == 0, 0, jnp.minimum(lj, n_x - 1) * rpt)
        return base + idx

    tile_spec = pl.BlockSpec((TILE, D_RNN), lambda b, j: (phys_tile(b, j), 0))
    prev_spec = pl.BlockSpec((8, D_RNN), lambda b, j: (prev_blk(b, j), 0))
    next_spec = pl.BlockSpec((8, D_RNN), lambda b, j: (next_blk(b, j), 0))
    return tile_spec, prev_spec, next_spec


def _gate_specs():
    return [
        _const_spec((4, D_RNN)),
        _const_spec((1, D_RNN)),
        _const_spec((N_RG_BLOCKS, RG_BW, RG_BW)),
        _const_spec((N_RG_BLOCKS, RG_BW, RG_BW)),
        _const_spec((1, D_RNN)),
        _const_spec((1, D_RNN)),
        _const_spec((1, D_RNN)),
    ]


def _scan_fwd(xr, batch, conv_w, conv_b, wr, wi, br, bi, lam):
    tp = xr.shape[0]
    n_tiles = tp // TILE // batch
    tile_spec, prev_spec, next_spec = _scan_specs(batch, n_tiles, reverse=False)
    return pl.pallas_call(
        functools.partial(_scan_fwd_body, n_tiles=n_tiles),
        grid=(batch, n_tiles),
        in_specs=[tile_spec, prev_spec, next_spec] + _gate_specs(),
        out_specs=tile_spec,
        out_shape=jax.ShapeDtypeStruct((tp, D_RNN), F32),
        scratch_shapes=[pltpu.VMEM((TILE, D_RNN), F32), pltpu.VMEM((TILE, D_RNN), F32),
                        pltpu.VMEM((1, D_RNN), F32)],
        compiler_params=_cparams(("arbitrary", "arbitrary")),
        name="scan_fwd",
    )(xr, xr, xr, conv_w, conv_b, wr, wi, br, bi, lam)


def _scan_bwd(xr, batch, conv_w, conv_b, wr, wi, br, bi, lam, hf, gate, h, w_out, g):
    tp = xr.shape[0]
    n_tiles = tp // TILE // batch
    tile_spec, prev_spec, next_spec = _scan_specs(batch, n_tiles, reverse=True)
    return pl.pallas_call(
        functools.partial(_scan_bwd_body, n_tiles=n_tiles),
        grid=(batch, n_tiles),
        in_specs=[tile_spec, prev_spec, next_spec] + _gate_specs()
        + [tile_spec, tile_spec, tile_spec, _const_spec((D_RNN, D_MODEL)), _const_spec((1, D_MODEL))],
        out_specs=tile_spec,
        out_shape=jax.ShapeDtypeStruct((tp, D_MODEL), F32),
        scratch_shapes=[pltpu.VMEM((TILE, D_RNN), F32), pltpu.VMEM((TILE, D_RNN), F32),
                        pltpu.VMEM((TILE, D_RNN), F32), pltpu.VMEM((1, D_RNN), F32)],
        compiler_params=_cparams(("arbitrary", "arbitrary")),
        name="scan_bwd",
    )(xr, xr, xr, conv_w, conv_b, wr, wi, br, bi, lam, hf, gate, h, w_out, g.reshape(1, -1))


def _rope_tables_t(n_tok, lp, batch):
    n = jnp.arange(n_tok, dtype=jnp.int32)
    row = (n // GRID_W).astype(F32)
    col = (n % GRID_W).astype(F32)
    zeros = jnp.zeros((lp - n_tok,), F32)
    row = jnp.concatenate([row, zeros])
    col = jnp.concatenate([col, zeros])
    inv = 1.0 / (ROPE_THETA ** (jnp.arange(0, ROPE_AXIS, 2, dtype=F32) / ROPE_AXIS))
    theta_t = jnp.concatenate([inv[:, None] * row[None, :], inv[:, None] * col[None, :]], axis=0)
    theta_t = jnp.tile(theta_t, (1, batch))
    return jnp.cos(theta_t), jnp.sin(theta_t)


def kernel(x, meta_tokens, norm_gains, ffn_w_in, ffn_w_out, attn_w_qkv, attn_q_gain, attn_k_gain,
           attn_w_o, rec_w_in, rec_conv_w, rec_conv_b, rec_gate_w, rec_gate_b, rec_lambda, rec_w_out):
    batch, n_tok, d = x.shape
    assert d == D_MODEL and n_tok % TILE == 0 and N_META % 8 == 0 and N_META + 8 <= TILE
    lp = n_tok + TILE
    assert TILE % FFN_SUB == 0 and n_tok % (ATT_UNROLL * ATT_TK) == 0 and N_META <= META_KEY_ROWS <= TILE
    depth = norm_gains.shape[0]

    meta = jnp.broadcast_to(meta_tokens.astype(x.dtype)[None], (batch, N_META, d))
    pad = jnp.zeros((batch, TILE - N_META, d), x.dtype)
    h = jnp.concatenate([x, meta, pad], axis=1).reshape(batch * lp, d)
    cos_t, sin_t = _rope_tables_t(n_tok, lp, batch)

    w_in = ffn_w_in.astype(BF16)
    w_out = ffn_w_out.astype(BF16)
    for layer in range(depth):
        g = norm_gains[layer]
        h = _ffn(h, g[0], g[1], w_in, w_out, layer, 0)
        j = layer // 2
        if layer % 2 == 0:
            q_t, k, v_t = _attn_qkv(h, g[2], attn_w_qkv[j].T.astype(BF16), attn_q_gain[j],
                                    attn_k_gain[j], cos_t, sin_t)
            o = _attention(q_t, k, v_t, batch, n_tok, lp)
            h = _proj_ffn(o, h, attn_w_o[j].astype(BF16), g[3], g[4], g[5], w_in, w_out, layer, 1)
        else:
            gate, xr = _rec_in(h, g[2], rec_w_in[j].astype(BF16))
            gw = rec_gate_w[j].astype(BF16)
            gb = rec_gate_b[j]
            lam = rec_lambda[j]
            cw = rec_conv_w[j]
            cb = rec_conv_b[j].reshape(1, -1)
            hf = _scan_fwd(xr, batch, cw, cb, gw[0, 0], gw[0, 1], gb[0, 0].reshape(1, -1),
                           gb[0, 1].reshape(1, -1), lam[0].reshape(1, -1))
            h = _scan_bwd(xr, batch, cw, cb, gw[1, 0], gw[1, 1], gb[1, 0].reshape(1, -1),
                          gb[1, 1].reshape(1, -1), lam[1].reshape(1, -1), hf, gate, h,
                          rec_w_out[j].astype(BF16), g[3])
            h = _ffn(h, g[4], g[5], w_in, w_out, layer, 1)
    return h.reshape(batch, lp, d)[:, :n_tok]
```

```python
import functools

import jax
import jax.numpy as jnp
from jax import lax
from jax.experimental import pallas as pl
from jax.experimental.pallas import tpu as pltpu

D_MODEL = 1024
N_META = 16
GRID_W = 64
HEAD_DIM = 64
N_Q_HEADS = 16
N_KV_HEADS = 4
GROUP = 4
QKV_WIDTH = 1536
ROPE_AXIS = 32
ROPE_THETA = 10000.0
D_RNN = 1024
N_RG_BLOCKS = 4
RG_BW = 256
RG_C = 8.0
D_FF = 2816
NORM_EPS = 1e-6

VMEM_LIMIT_BYTES = 56 * 1024 * 1024

TILE = 256
ATT_TK = 1024
ATT_NQ = 13
ATT_UNROLL = 8
LOG2E = 1.4426950408889634
FFN_TM = 1280
FFN_SUB = 256
META_KEY_ROWS = 128
V_ROWS = 80
NEG_BIG = -1e30

BF16 = jnp.bfloat16
F32 = jnp.float32


def _cparams(sem):
    return pltpu.CompilerParams(dimension_semantics=sem, vmem_limit_bytes=VMEM_LIMIT_BYTES)


def _rms_rows(x, g):
    return x * lax.rsqrt(jnp.mean(x * x, axis=-1, keepdims=True) + NORM_EPS) * g


def _row_block(tp):
    return max(m for m in range(FFN_SUB, FFN_TM + 1, FFN_SUB) if tp % m == 0)


def _const_spec(shape):
    zeros = (0,) * len(shape)
    return pl.BlockSpec(shape, lambda *_: zeros)


FF_CHUNKS = (768, 768, 768, 512)


def _ffn_rows(load_x, n_sub, sub, gpre, gpost, win_ref, wout_ref, o_ref):
    def prep(s):
        x = load_x(s)
        return x, _rms_rows(x, gpre).astype(BF16)

    def finish(s, x, acc):
        o_ref[s * sub:(s + 1) * sub, :] = x + 0.5 * _rms_rows(acc, gpost)

    cur = prep(0)
    prev = None
    for s in range(n_sub):
        x, xn = cur
        acc = None
        c0 = 0
        for ci, ck in enumerate(FF_CHUNKS):
            a = jnp.dot(xn, win_ref[:, c0:c0 + ck], preferred_element_type=F32)
            b = jnp.dot(xn, win_ref[:, D_FF + c0:D_FF + c0 + ck], preferred_element_type=F32)
            g = (a * jax.nn.sigmoid(a) * b).astype(BF16)
            d = jnp.dot(g, wout_ref[c0:c0 + ck, :], preferred_element_type=F32)
            acc = d if acc is None else acc + d
            c0 += ck
            if ci == 0:
                if s + 1 < n_sub:
                    cur = prep(s + 1)
                if prev is not None:
                    finish(*prev)
        prev = (s, x, acc)
    finish(*prev)


def _ffn_body(h_ref, gpre_ref, gpost_ref, win_ref, wout_ref, o_ref):
    def load_x(s):
        return h_ref[s * FFN_SUB:(s + 1) * FFN_SUB, :]

    _ffn_rows(load_x, h_ref.shape[0] // FFN_SUB, FFN_SUB, gpre_ref[...], gpost_ref[...], win_ref, wout_ref, o_ref)


def _proj_ffn_body(a_ref, h_ref, wp_ref, gp_ref, gpre_ref, gpost_ref, win_ref, wout_ref, o_ref):
    def load_x(s):
        rows = slice(s * FFN_SUB, (s + 1) * FFN_SUB)
        m = jnp.dot(a_ref[rows, :], wp_ref[...], preferred_element_type=F32)
        return h_ref[rows, :] + _rms_rows(m, gp_ref[...])

    _ffn_rows(load_x, h_ref.shape[0] // FFN_SUB, FFN_SUB, gpre_ref[...], gpost_ref[...], win_ref, wout_ref, o_ref)


def _ffn_weight_specs(layer, idx):
    return [
        pl.BlockSpec((None, None, D_MODEL, 2 * D_FF), lambda i: (layer, idx, 0, 0),
                     pipeline_mode=pl.Buffered(1)),
        pl.BlockSpec((None, None, D_FF, D_MODEL), lambda i: (layer, idx, 0, 0),
                     pipeline_mode=pl.Buffered(1)),
    ]


def _ffn(h, gpre, gpost, w_in, w_out, layer, idx):
    tp = h.shape[0]
    tm = _row_block(tp)
    row_spec = pl.BlockSpec((tm, D_MODEL), lambda i: (i, 0))
    return pl.pallas_call(
        _ffn_body,
        grid=(tp // tm,),
        in_specs=[row_spec, _const_spec((1, D_MODEL)), _const_spec((1, D_MODEL))]
        + _ffn_weight_specs(layer, idx),
        out_specs=row_spec,
        out_shape=jax.ShapeDtypeStruct(h.shape, F32),
        compiler_params=_cparams(("arbitrary",)),
        name="ffn",
    )(h, gpre.reshape(1, -1), gpost.reshape(1, -1), w_in, w_out)


def _proj_ffn(a, h, wp, gp, gpre, gpost, w_in, w_out, layer, idx):
    tp, kdim = a.shape
    tm = _row_block(tp)
    row_spec = pl.BlockSpec((tm, D_MODEL), lambda i: (i, 0))
    return pl.pallas_call(
        _proj_ffn_body,
        grid=(tp // tm,),
        in_specs=[pl.BlockSpec((tm, kdim), lambda i: (i, 0)), row_spec,
                  pl.BlockSpec((kdim, D_MODEL), lambda i: (0, 0), pipeline_mode=pl.Buffered(1)),
                  _const_spec((1, D_MODEL)), _const_spec((1, D_MODEL)), _const_spec((1, D_MODEL))]
        + _ffn_weight_specs(layer, idx),
        out_specs=row_spec,
        out_shape=jax.ShapeDtypeStruct(h.shape, F32),
        compiler_params=_cparams(("arbitrary",)),
        name="proj_ffn",
    )(a, h, wp, gp.reshape(1, -1), gpre.reshape(1, -1), gpost.reshape(1, -1), w_in, w_out)


def _head_norm_rope(blk, gain, cos, sin):
    y = blk * lax.rsqrt(jnp.mean(blk * blk, axis=0, keepdims=True) + NORM_EPS) * gain
    x1 = y[:ROPE_AXIS]
    x2 = y[ROPE_AXIS:]
    return jnp.concatenate([x1 * cos - x2 * sin, x1 * sin + x2 * cos], axis=0)


def _qkv_body(h_ref, g_ref, wt_ref, qg_ref, kg_ref, cos_ref, sin_ref, qt_ref, k_ref, vt_ref):
    n_sub = k_ref.shape[0]
    qg = qg_ref[...]
    kg = kg_ref[...]
    ones_rows = (lax.broadcasted_iota(jnp.int32, (V_ROWS - HEAD_DIM, TILE), 0) == 0).astype(BF16)

    def project(s):
        hn = _rms_rows(h_ref[s * TILE:(s + 1) * TILE, :], g_ref[...]).astype(BF16)
        return lax.dot_general(wt_ref[...], hn, (((1,), (1,)), ((), ())),
                               preferred_element_type=F32)

    def emit(s, qkv_t):
        cols = slice(s * TILE, (s + 1) * TILE)
        cos = cos_ref[:, cols]
        sin = sin_ref[:, cols]
        for hq in range(N_Q_HEADS):
            blk = qkv_t[hq * HEAD_DIM:(hq + 1) * HEAD_DIM]
            q = _head_norm_rope(blk, qg, cos, sin) * (HEAD_DIM ** -0.5 * LOG2E)
            qt_ref[hq * HEAD_DIM:(hq + 1) * HEAD_DIM, cols] = q.astype(BF16)
        k_off = N_Q_HEADS * HEAD_DIM
        k_heads = []
        for hk in range(N_KV_HEADS):
            blk = qkv_t[k_off + hk * HEAD_DIM:k_off + (hk + 1) * HEAD_DIM]
            k_heads.append(_head_norm_rope(blk, kg, cos, sin))
        k_t = jnp.concatenate(k_heads, axis=0)
        k_ref[s] = k_t.T.astype(BF16)
        v_off = k_off + N_KV_HEADS * HEAD_DIM
        for hk in range(N_KV_HEADS):
            v = qkv_t[v_off + hk * HEAD_DIM:v_off + (hk + 1) * HEAD_DIM].astype(BF16)
            vt_ref[hk, s] = jnp.concatenate([v, ones_rows], axis=0)

    cur = project(0)
    for s in range(n_sub):
        nxt = project(s + 1) if s + 1 < n_sub else None
        emit(s, cur)
        cur = nxt


def _attn_qkv(h, g, w_t, q_gain, k_gain, cos_t, sin_t):
    tp = h.shape[0]
    nt = tp // TILE
    tm = _row_block(tp)
    n_sub = tm // TILE
    return pl.pallas_call(
        _qkv_body,
        grid=(tp // tm,),
        in_specs=[
            pl.BlockSpec((tm, D_MODEL), lambda i: (i, 0)),
            _const_spec((1, D_MODEL)),
            _const_spec((QKV_WIDTH, D_MODEL)),
            _const_spec((HEAD_DIM, 1)),
            _const_spec((HEAD_DIM, 1)),
            pl.BlockSpec((ROPE_AXIS, tm), lambda i: (0, i)),
            pl.BlockSpec((ROPE_AXIS, tm), lambda i: (0, i)),
        ],
        out_specs=[
            pl.BlockSpec((N_Q_HEADS * HEAD_DIM, tm), lambda i: (0, i)),
            pl.BlockSpec((n_sub, TILE, N_KV_HEADS * HEAD_DIM), lambda i: (i, 0, 0)),
            pl.BlockSpec((N_KV_HEADS, n_sub, V_ROWS, TILE), lambda i: (0, i, 0, 0)),
        ],
        out_shape=[
            jax.ShapeDtypeStruct((N_Q_HEADS * HEAD_DIM, tp), BF16),
            jax.ShapeDtypeStruct((nt, TILE, N_KV_HEADS * HEAD_DIM), BF16),
            jax.ShapeDtypeStruct((N_KV_HEADS, nt, V_ROWS, TILE), BF16),
        ],
        compiler_params=_cparams(("arbitrary",)),
        name="attn_qkv",
    )(h, g.reshape(1, -1), w_t, q_gain.reshape(-1, 1), k_gain.reshape(-1, 1), cos_t, sin_t)


def _attn_body(qt_ref, k_ref, vt_ref, o_ref, qaug_ref, vc_ref, sa_ref, sb_ref, *, n_chunks, n_q, meta_rows):
    kvh = pl.program_id(1)
    tq = TILE
    tpc = ATT_TK // TILE
    meta_tile = n_chunks * tpc
    n_steps = n_q * n_chunks

    qaug_ref[...] = jnp.zeros(qaug_ref.shape, BF16)
    row0 = pl.multiple_of(kvh * HEAD_DIM, HEAD_DIM)
    for j in range(n_q):
        for g in range(GROUP):
            qaug_ref[j * GROUP + g, pl.ds(row0, HEAD_DIM), :] = (
                qt_ref[g * HEAD_DIM:(g + 1) * HEAD_DIM, j * tq:(j + 1) * tq])

    def scores(t, g, s_ref):
        qi = t // n_chunks
        c = t % n_chunks
        kc = k_ref[pl.ds(c * tpc, tpc)].reshape(ATT_TK, N_KV_HEADS * HEAD_DIM)
        s = jnp.dot(kc, qaug_ref[qi * GROUP + g], preferred_element_type=F32)
        s_ref[g] = s
        return jnp.max(s, axis=0, keepdims=True)

    @pl.when(pl.program_id(2) == 0)
    def _():
        for c in range(n_chunks):
            for u in range(tpc):
                vc_ref[c, :, u * TILE:(u + 1) * TILE] = vt_ref[0, c * tpc + u]

    def v_chunk(c):
        return vc_ref[c]

    def softmax_pv(vc, g, s_ref, cmax, m, acc):
        m_new = jnp.maximum(m, cmax)
        alpha = jnp.exp2(m - m_new)
        p = jnp.exp2(s_ref[g] - m_new).astype(BF16)
        acc = alpha * acc + jnp.dot(vc, p, preferred_element_type=F32)
        return m_new, acc

    def finalize(qi, state):
        k_meta = k_ref[meta_tile, 0:META_KEY_ROWS, :]
        v_meta = vt_ref[0, meta_tile, :, 0:META_KEY_ROWS]
        for g in range(GROUP):
            m, acc = state[g]
            s = jnp.dot(k_meta, qaug_ref[qi * GROUP + g], preferred_element_type=F32)
            rows = lax.broadcasted_iota(jnp.int32, s.shape, 0)
            s = jnp.where(rows < meta_rows, s, NEG_BIG)
            m_new = jnp.maximum(m, jnp.max(s, axis=0, keepdims=True))
            p = jnp.exp2(s - m_new).astype(BF16)
            acc = jnp.exp2(m - m_new) * acc + jnp.dot(v_meta, p, preferred_element_type=F32)
            out_t = acc[:HEAD_DIM] / acc[HEAD_DIM:HEAD_DIM + 1]
            o_ref[pl.ds(pl.multiple_of(qi * tq, tq), tq), g * HEAD_DIM:(g + 1) * HEAD_DIM] = (
                out_t.T.astype(BF16))

    def step(t_next, next_ref, cur_ref, cmaxes, state, reset):
        c_cur = (t_next - 1) % n_chunks
        vc = v_chunk(c_cur)
        new_cmax, new_state = [], []
        for g in range(GROUP):
            m, acc = state[g]
            if reset:
                first = c_cur == 0
                m = jnp.where(first, NEG_BIG, m)
                acc = jnp.where(first, 0.0, acc)
            new_cmax.append(scores(t_next, g, next_ref))
            new_state.append(softmax_pv(vc, g, cur_ref, cmaxes[g], m, acc))
        return tuple(new_cmax), tuple(new_state)

    bufs = (sa_ref, sb_ref)
    unroll = ATT_UNROLL
    assert unroll % 2 == 0 and n_chunks % unroll == 0

    def group(i, carry):
        cmaxes, state = carry
        t0 = unroll * i
        for u in range(1, unroll + 1):
            cmaxes, state = step(t0 + u, bufs[u % 2], bufs[(u - 1) % 2], cmaxes, state, reset=(u == 1))

        @pl.when((t0 + unroll - 1) % n_chunks == n_chunks - 1)
        def _():
            finalize((t0 + unroll - 1) // n_chunks, state)

        return cmaxes, state

    state = tuple((jnp.full((1, tq), NEG_BIG, F32), jnp.zeros((V_ROWS, tq), F32)) for _ in range(GROUP))
    cmaxes = tuple(scores(0, g, sa_ref) for g in range(GROUP))
    n_groups = n_steps // unroll - 1
    cmaxes, state = lax.fori_loop(0, n_groups, group, (cmaxes, state))
    for t in range(unroll * n_groups + 1, n_steps):
        cmaxes, state = step(t, bufs[t % 2], bufs[(t - 1) % 2], cmaxes, state, reset=True)
    vc_last = v_chunk(n_chunks - 1)
    state = tuple(softmax_pv(vc_last, g, bufs[(n_steps - 1) % 2], cmaxes[g], *state[g])
                  for g in range(GROUP))
    finalize(n_q - 1, state)


def _attention(q_t, k, v_t, batch, n_tok, lp):
    tp = q_t.shape[1]
    n_chunks = n_tok // ATT_TK
    tiles_per_batch = lp // TILE
    n_q = max(d for d in range(1, ATT_NQ + 1) if tiles_per_batch % d == 0)
    q_steps = tiles_per_batch // n_q
    body = functools.partial(_attn_body, n_chunks=n_chunks, n_q=n_q, meta_rows=N_META)
    return pl.pallas_call(
        body,
        grid=(batch, N_KV_HEADS, q_steps),
        in_specs=[
            pl.BlockSpec((GROUP * HEAD_DIM, n_q * TILE), lambda b, h, i: (h, b * q_steps + i)),
            pl.BlockSpec((tiles_per_batch, TILE, N_KV_HEADS * HEAD_DIM), lambda b, h, i: (b, 0, 0)),
            pl.BlockSpec((1, tiles_per_batch, V_ROWS, TILE), lambda b, h, i: (h, b, 0, 0)),
        ],
        out_specs=pl.BlockSpec((n_q * TILE, GROUP * HEAD_DIM), lambda b, h, i: (b * q_steps + i, h)),
        out_shape=jax.ShapeDtypeStruct((tp, N_Q_HEADS * HEAD_DIM), BF16),
        scratch_shapes=[pltpu.VMEM((n_q * GROUP, N_KV_HEADS * HEAD_DIM, TILE), BF16),
                        pltpu.VMEM((n_chunks, V_ROWS, ATT_TK), BF16),
                        pltpu.VMEM((GROUP, ATT_TK, TILE), F32),
                        pltpu.VMEM((GROUP, ATT_TK, TILE), F32)],
        compiler_params=_cparams(("arbitrary", "arbitrary", "arbitrary")),
        name="attention",
    )(q_t, k, v_t)


def _rec_in_body(h_ref, g_ref, w_ref, gate_ref, xr_ref):
    hn = _rms_rows(h_ref[...], g_ref[...]).astype(BF16)
    u = jnp.dot(hn, w_ref[...], preferred_element_type=F32)
    x = u[:, :D_RNN]
    cdf = 0.5 * (1.0 + jnp.tanh(0.7978845608028654 * (x + 0.044715 * (x * x * x))))
    gate_ref[...] = x * cdf
    xr_ref[...] = u[:, D_RNN:]


def _rec_in(h, g, w):
    tp = h.shape[0]
    tm = _row_block(tp)
    spec = pl.BlockSpec((tm, D_MODEL), lambda i: (i, 0))
    return pl.pallas_call(
        _rec_in_body,
        grid=(tp // tm,),
        in_specs=[spec, _const_spec((1, D_MODEL)), _const_spec((D_MODEL, 2 * D_RNN))],
        out_specs=[spec, spec],
        out_shape=[jax.ShapeDtypeStruct((tp, D_RNN), F32)] * 2,
        compiler_params=_cparams(("arbitrary",)),
        name="rec_in",
    )(h, g.reshape(1, -1), w)


def _rglru_coeffs(x, prev8, next8, cw_ref, cb_ref, wr_ref, wi_ref, br_ref, bi_ref, lam_ref):
    tt = x.shape[0]
    ext = jnp.concatenate([prev8, x, next8], axis=0)
    n_ext = tt + 16
    xm1 = pltpu.roll(ext, 1, axis=0)[8:8 + tt]
    xm2 = pltpu.roll(ext, 2, axis=0)[8:8 + tt]
    xp1 = pltpu.roll(ext, n_ext - 1, axis=0)[8:8 + tt]
    cw = cw_ref[...]
    xc = cb_ref[...] + xm2 * cw[0:1] + xm1 * cw[1:2] + x * cw[2:3] + xp1 * cw[3:4]
    k = (-RG_C * LOG2E) * jnp.logaddexp(-lam_ref[...], 0.0)
    r_parts = []
    i_parts = []
    for blk in range(N_RG_BLOCKS):
        xb = xc[:, blk * RG_BW:(blk + 1) * RG_BW].astype(BF16)
        r_parts.append(jnp.dot(xb, wr_ref[blk], preferred_element_type=F32))
        i_parts.append(jnp.dot(xb, wi_ref[blk], preferred_element_type=F32))
    r = jax.nn.sigmoid(jnp.concatenate(r_parts, axis=1) + br_ref[...])
    i = jax.nn.sigmoid(jnp.concatenate(i_parts, axis=1) + bi_ref[...])
    a = jnp.exp2(r * k)
    b = jnp.sqrt(1.0 - a * a) * i * xc
    return a, b


def _store_coeffs(x_ref, prev_ref, next_ref, is_tail, has_prev, has_next, a_s, b_s, gate_refs):
    @pl.when(jnp.logical_not(is_tail))
    def _():
        prev8 = jnp.where(has_prev, prev_ref[...], 0.0)
        next8 = jnp.where(has_next, next_ref[...], 0.0)
        a, b = _rglru_coeffs(x_ref[...], prev8, next8, *gate_refs)
        a_s[...] = a
        b_s[...] = b

    @pl.when(is_tail)
    def _():
        x = x_ref[...]
        zeros8 = jnp.zeros((8, x.shape[1]), F32)
        x = jnp.concatenate([x[:N_META], next_ref[...], x[N_META + 8:]], axis=0)
        a, b = _rglru_coeffs(x, zeros8, zeros8, *gate_refs)
        valid = lax.broadcasted_iota(jnp.int32, x.shape, 0) < N_META
        a_s[...] = jnp.where(valid, a, 1.0)
        b_s[...] = jnp.where(valid, b, 0.0)


def _scan_fwd_body(x_ref, prev_ref, next_ref, cw_ref, cb_ref, wr_ref, wi_ref, br_ref, bi_ref,
                   lam_ref, hf_ref, a_s, b_s, carry_s, *, n_tiles):
    j = pl.program_id(1)
    _store_coeffs(x_ref, prev_ref, next_ref, is_tail=j == 0, has_prev=j > 0, has_next=j < n_tiles - 1,
                  a_s=a_s, b_s=b_s,
                  gate_refs=(cw_ref, cb_ref, wr_ref, wi_ref, br_ref, bi_ref, lam_ref))

    @pl.when(j == 0)
    def _():
        carry_s[...] = jnp.zeros(carry_s.shape, F32)

    def step(r, c):
        c = a_s[pl.ds(r, 1), :] * c + b_s[pl.ds(r, 1), :]
        hf_ref[pl.ds(r, 1), :] = c
        return c

    carry_s[...] = lax.fori_loop(0, TILE, step, carry_s[...], unroll=8)


def _scan_bwd_body(x_ref, prev_ref, next_ref, cw_ref, cb_ref, wr_ref, wi_ref, br_ref, bi_ref,
                   lam_ref, hf_ref, gate_ref, h_ref, wout_ref, g_ref, o_ref,
                   a_s, b_s, hb_s, carry_s, *, n_tiles):
    j = pl.program_id(1)
    _store_coeffs(x_ref, prev_ref, next_ref, is_tail=j == n_tiles - 1, has_prev=j < n_tiles - 1,
                  has_next=j > 0, a_s=a_s, b_s=b_s,
                  gate_refs=(cw_ref, cb_ref, wr_ref, wi_ref, br_ref, bi_ref, lam_ref))

    @pl.when(j == 0)
    def _():
        carry_s[...] = jnp.zeros(carry_s.shape, F32)

    def step(t, c):
        r = TILE - 1 - t
        c = a_s[pl.ds(r, 1), :] * c + b_s[pl.ds(r, 1), :]
        hb_s[pl.ds(r, 1), :] = c
        return c

    carry_s[...] = lax.fori_loop(0, TILE, step, carry_s[...], unroll=8)
    y = ((hf_ref[...] + hb_s[...]) * gate_ref[...]).astype(BF16)
    m = jnp.dot(y, wout_ref[...], preferred_element_type=F32)
    o_ref[...] = h_ref[...] + _rms_rows(m, g_ref[...])


def _scan_specs(batch, n_tiles, reverse):
    n_x = n_tiles - 1
    rpt = TILE // 8

    def logical(j):
        return (n_tiles - 1 - j) if reverse else j

    def phys_tile(b, j):
        lj = logical(j)
        return b * n_tiles + jnp.where(lj == 0, n_x, lj - 1)

    def prev_blk(b, j):
        lj = logical(j)
        base = b * n_tiles * rpt
        idx = jnp.where(lj <= 1, n_x * rpt + 1, (lj - 1) * rpt - 1)
        return base + idx

    def next_blk(b, j):
        lj = logical(j)
        base = b * n_tiles * rpt
        idx = jnp.where(lj == 0, 0, jnp.minimum(lj, n_x - 1) * rpt)
        return base + idx

    tile_spec = pl.BlockSpec((TILE, D_RNN), lambda b, j: (phys_tile(b, j), 0))
    prev_spec = pl.BlockSpec((8, D_RNN), lambda b, j: (prev_blk(b, j), 0))
    next_spec = pl.BlockSpec((8, D_RNN), lambda b, j: (next_blk(b, j), 0))
    return tile_spec, prev_spec, next_spec


def _gate_specs():
    return [
        _const_spec((4, D_RNN)),
        _const_spec((1, D_RNN)),
        _const_spec((N_RG_BLOCKS, RG_BW, RG_BW)),
        _const_spec((N_RG_BLOCKS, RG_BW, RG_BW)),
        _const_spec((1, D_RNN)),
        _const_spec((1, D_RNN)),
        _const_spec((1, D_RNN)),
    ]


def _scan_fwd(xr, batch, conv_w, conv_b, wr, wi, br, bi, lam):
    tp = xr.shape[0]
    n_tiles = tp // TILE // batch
    tile_spec, prev_spec, next_spec = _scan_specs(batch, n_tiles, reverse=False)
    return pl.pallas_call(
        functools.partial(_scan_fwd_body, n_tiles=n_tiles),
        grid=(batch, n_tiles),
        in_specs=[tile_spec, prev_spec, next_spec] + _gate_specs(),
        out_specs=tile_spec,
        out_shape=jax.ShapeDtypeStruct((tp, D_RNN), F32),
        scratch_shapes=[pltpu.VMEM((TILE, D_RNN), F32), pltpu.VMEM((TILE, D_RNN), F32),
                        pltpu.VMEM((1, D_RNN), F32)],
        compiler_params=_cparams(("arbitrary", "arbitrary")),
        name="scan_fwd",
    )(xr, xr, xr, conv_w, conv_b, wr, wi, br, bi, lam)


def _scan_bwd(xr, batch, conv_w, conv_b, wr, wi, br, bi, lam, hf, gate, h, w_out, g):
    tp = xr.shape[0]
    n_tiles = tp // TILE // batch
    tile_spec, prev_spec, next_spec = _scan_specs(batch, n_tiles, reverse=True)
    return pl.pallas_call(
        functools.partial(_scan_bwd_body, n_tiles=n_tiles),
        grid=(batch, n_tiles),
        in_specs=[tile_spec, prev_spec, next_spec] + _gate_specs()
        + [tile_spec, tile_spec, tile_spec, _const_spec((D_RNN, D_MODEL)), _const_spec((1, D_MODEL))],
        out_specs=tile_spec,
        out_shape=jax.ShapeDtypeStruct((tp, D_MODEL), F32),
        scratch_shapes=[pltpu.VMEM((TILE, D_RNN), F32), pltpu.VMEM((TILE, D_RNN), F32),
                        pltpu.VMEM((TILE, D_RNN), F32), pltpu.VMEM((1, D_RNN), F32)],
        compiler_params=_cparams(("arbitrary", "arbitrary")),
        name="scan_bwd",
    )(xr, xr, xr, conv_w, conv_b, wr, wi, br, bi, lam, hf, gate, h, w_out, g.reshape(1, -1))


def _rope_tables_t(n_tok, lp, batch):
    n = jnp.arange(n_tok, dtype=jnp.int32)
    row = (n // GRID_W).astype(F32)
    col = (n % GRID_W).astype(F32)
    zeros = jnp.zeros((lp - n_tok,), F32)
    row = jnp.concatenate([row, zeros])
    col = jnp.concatenate([col, zeros])
    inv = 1.0 / (ROPE_THETA ** (jnp.arange(0, ROPE_AXIS, 2, dtype=F32) / ROPE_AXIS))
    theta_t = jnp.concatenate([inv[:, None] * row[None, :], inv[:, None] * col[None, :]], axis=0)
    theta_t = jnp.tile(theta_t, (1, batch))
    return jnp.cos(theta_t), jnp.sin(theta_t)


def kernel(x, meta_tokens, norm_gains, ffn_w_in, ffn_w_out, attn_w_qkv, attn_q_gain, attn_k_gain,
           attn_w_o, rec_w_in, rec_conv_w, rec_conv_b, rec_gate_w, rec_gate_b, rec_lambda, rec_w_out):
    batch, n_tok, d = x.shape
    assert d == D_MODEL and n_tok % TILE == 0 and N_META % 8 == 0 and N_META + 8 <= TILE
    lp = n_tok + TILE
    assert TILE % FFN_SUB == 0 and n_tok % (ATT_UNROLL * ATT_TK) == 0 and N_META <= META_KEY_ROWS <= TILE
    depth = norm_gains.shape[0]

    meta = jnp.broadcast_to(meta_tokens.astype(x.dtype)[None], (batch, N_META, d))
    pad = jnp.zeros((batch, TILE - N_META, d), x.dtype)
    h = jnp.concatenate([x, meta, pad], axis=1).reshape(batch * lp, d)
    cos_t, sin_t = _rope_tables_t(n_tok, lp, batch)

    w_in = ffn_w_in.astype(BF16)
    w_out = ffn_w_out.astype(BF16)
    for layer in range(depth):
        g = norm_gains[layer]
        h = _ffn(h, g[0], g[1], w_in, w_out, layer, 0)
        j = layer // 2
        if layer % 2 == 0:
            q_t, k, v_t = _attn_qkv(h, g[2], attn_w_qkv[j].T.astype(BF16), attn_q_gain[j],
                                    attn_k_gain[j], cos_t, sin_t)
            o = _attention(q_t, k, v_t, batch, n_tok, lp)
            h = _proj_ffn(o, h, attn_w_o[j].astype(BF16), g[3], g[4], g[5], w_in, w_out, layer, 1)
        else:
            gate, xr = _rec_in(h, g[2], rec_w_in[j].astype(BF16))
            gw = rec_gate_w[j].astype(BF16)
            gb = rec_gate_b[j]
            lam = rec_lambda[j]
            cw = rec_conv_w[j]
            cb = rec_conv_b[j].reshape(1, -1)
            hf = _scan_fwd(xr, batch, cw, cb, gw[0, 0], gw[0, 1], gb[0, 0].reshape(1, -1),
                           gb[0, 1].reshape(1, -1), lam[0].reshape(1, -1))
            h = _scan_bwd(xr, batch, cw, cb, gw[1, 0], gw[1, 1], gb[1, 0].reshape(1, -1),
                          gb[1, 1].reshape(1, -1), lam[1].reshape(1, -1), hf, gate, h,
                          rec_w_out[j].astype(BF16), g[3])
            h = _ffn(h, g[4], g[5], w_in, w_out, layer, 1)
    return h.reshape(batch, lp, d)[:, :n_tok]
```

```python
import functools

import jax
import jax.numpy as jnp
from jax import lax
from jax.experimental import pallas as pl
from jax.experimental.pallas import tpu as pltpu

D_MODEL = 1024
N_META = 16
GRID_W = 64
HEAD_DIM = 64
N_Q_HEADS = 16
N_KV_HEADS = 4
GROUP = 4
QKV_WIDTH = 1536
ROPE_AXIS = 32
ROPE_THETA = 10000.0
D_RNN = 1024
N_RG_BLOCKS = 4
RG_BW = 256
RG_C = 8.0
D_FF = 2816
NORM_EPS = 1e-6

VMEM_LIMIT_BYTES = 56 * 1024 * 1024

TILE = 256
ATT_TK = 1024
ATT_NQ = 13
ATT_UNROLL = 8
LOG2E = 1.4426950408889634
FFN_TM = 1280
FFN_SUB = 256
META_KEY_ROWS = 128
V_ROWS = 80
NEG_BIG = -1e30

BF16 = jnp.bfloat16
F32 = jnp.float32


def _cparams(sem):
    return pltpu.CompilerParams(dimension_semantics=sem, vmem_limit_bytes=VMEM_LIMIT_BYTES)


def _rms_rows(x, g):
    return x * lax.rsqrt(jnp.mean(x * x, axis=-1, keepdims=True) + NORM_EPS) * g


def _row_block(tp):
    return max(m for m in range(FFN_SUB, FFN_TM + 1, FFN_SUB) if tp % m == 0)


def _const_spec(shape):
    zeros = (0,) * len(shape)
    return pl.BlockSpec(shape, lambda *_: zeros)


FF_CHUNKS = (768, 768, 768, 512)


def _ffn_rows(load_x, n_sub, sub, gpre, gpost, win_ref, wout_ref, o_ref):
    def prep(s):
        x = load_x(s)
        return x, _rms_rows(x, gpre).astype(BF16)

    def finish(s, x, acc):
        o_ref[s * sub:(s + 1) * sub, :] = x + 0.5 * _rms_rows(acc, gpost)

    cur = prep(0)
    prev = None
    for s in range(n_sub):
        x, xn = cur
        acc = None
        c0 = 0
        for ci, ck in enumerate(FF_CHUNKS):
            a = jnp.dot(xn, win_ref[:, c0:c0 + ck], preferred_element_type=F32)
            b = jnp.dot(xn, win_ref[:, D_FF + c0:D_FF + c0 + ck], preferred_element_type=F32)
            g = (a * jax.nn.sigmoid(a) * b).astype(BF16)
            d = jnp.dot(g, wout_ref[c0:c0 + ck, :], preferred_element_type=F32)
            acc = d if acc is None else acc + d
            c0 += ck
            if ci == 0:
                if s + 1 < n_sub:
                    cur = prep(s + 1)
                if prev is not None:
                    finish(*prev)
        prev = (s, x, acc)
    finish(*prev)


def _ffn_body(h_ref, gpre_ref, gpost_ref, win_ref, wout_ref, o_ref):
    def load_x(s):
        return h_ref[s * FFN_SUB:(s + 1) * FFN_SUB, :]

    _ffn_rows(load_x, h_ref.shape[0] // FFN_SUB, FFN_SUB, gpre_ref[...], gpost_ref[...], win_ref, wout_ref, o_ref)


def _proj_ffn_body(a_ref, h_ref, wp_ref, gp_ref, gpre_ref, gpost_ref, win_ref, wout_ref, o_ref):
    def load_x(s):
        rows = slice(s * FFN_SUB, (s + 1) * FFN_SUB)
        m = jnp.dot(a_ref[rows, :], wp_ref[...], preferred_element_type=F32)
        return h_ref[rows, :] + _rms_rows(m, gp_ref[...])

    _ffn_rows(load_x, h_ref.shape[0] // FFN_SUB, FFN_SUB, gpre_ref[...], gpost_ref[...], win_ref, wout_ref, o_ref)


def _ffn_weight_specs(layer, idx):
    return [
        pl.BlockSpec((None, None, D_MODEL, 2 * D_FF), lambda i: (layer, idx, 0, 0),
                     pipeline_mode=pl.Buffered(1)),
        pl.BlockSpec((None, None, D_FF, D_MODEL), lambda i: (layer, idx, 0, 0),
                     pipeline_mode=pl.Buffered(1)),
    ]


def _ffn(h, gpre, gpost, w_in, w_out, layer, idx):
    tp = h.shape[0]
    tm = _row_block(tp)
    row_spec = pl.BlockSpec((tm, D_MODEL), lambda i: (i, 0))
    return pl.pallas_call(
        _ffn_body,
        grid=(tp // tm,),
        in_specs=[row_spec, _const_spec((1, D_MODEL)), _const_spec((1, D_MODEL))]
        + _ffn_weight_specs(layer, idx),
        out_specs=row_spec,
        out_shape=jax.ShapeDtypeStruct(h.shape, F32),
        compiler_params=_cparams(("arbitrary",)),
        name="ffn",
    )(h, gpre.reshape(1, -1), gpost.reshape(1, -1), w_in, w_out)


def _proj_ffn(a, h, wp, gp, gpre, gpost, w_in, w_out, layer, idx):
    tp, kdim = a.shape
    tm = _row_block(tp)
    row_spec = pl.BlockSpec((tm, D_MODEL), lambda i: (i, 0))
    return pl.pallas_call(
        _proj_ffn_body,
        grid=(tp // tm,),
        in_specs=[pl.BlockSpec((tm, kdim), lambda i: (i, 0)), row_spec,
                  pl.BlockSpec((kdim, D_MODEL), lambda i: (0, 0), pipeline_mode=pl.Buffered(1)),
                  _const_spec((1, D_MODEL)), _const_spec((1, D_MODEL)), _const_spec((1, D_MODEL))]
        + _ffn_weight_specs(layer, idx),
        out_specs=row_spec,
        out_shape=jax.ShapeDtypeStruct(h.shape, F32),
        compiler_params=_cparams(("arbitrary",)),
        name="proj_ffn",
    )(a, h, wp, gp.reshape(1, -1), gpre.reshape(1, -1), gpost.reshape(1, -1), w_in, w_out)


def _head_norm_rope(blk, gain, cos, sin):
    y = blk * lax.rsqrt(jnp.mean(blk * blk, axis=0, keepdims=True) + NORM_EPS) * gain
    x1 = y[:ROPE_AXIS]
    x2 = y[ROPE_AXIS:]
    return jnp.concatenate([x1 * cos - x2 * sin, x1 * sin + x2 * cos], axis=0)


def _qkv_body(h_ref, g_ref, wt_ref, qg_ref, kg_ref, cos_ref, sin_ref, qt_ref, k_ref, vt_ref):
    n_sub = k_ref.shape[0]
    qg = qg_ref[...]
    kg = kg_ref[...]
    ones_rows = (lax.broadcasted_iota(jnp.int32, (V_ROWS - HEAD_DIM, TILE), 0) == 0).astype(BF16)

    def project(s):
        hn = _rms_rows(h_ref[s * TILE:(s + 1) * TILE, :], g_ref[...]).astype(BF16)
        return lax.dot_general(wt_ref[...], hn, (((1,), (1,)), ((), ())),
                               preferred_element_type=F32)

    def emit(s, qkv_t):
        cols = slice(s * TILE, (s + 1) * TILE)
        cos = cos_ref[:, cols]
        sin = sin_ref[:, cols]
        for hq in range(N_Q_HEADS):
            blk = qkv_t[hq * HEAD_DIM:(hq + 1) * HEAD_DIM]
            q = _head_norm_rope(blk, qg, cos, sin) * (HEAD_DIM ** -0.5 * LOG2E)
            qt_ref[hq * HEAD_DIM:(hq + 1) * HEAD_DIM, cols] = q.astype(BF16)
        k_off = N_Q_HEADS * HEAD_DIM
        k_heads = []
        for hk in range(N_KV_HEADS):
            blk = qkv_t[k_off + hk * HEAD_DIM:k_off + (hk + 1) * HEAD_DIM]
            k_heads.append(_head_norm_rope(blk, kg, cos, sin))
        k_t = jnp.concatenate(k_heads, axis=0)
        k_ref[s] = k_t.T.astype(BF16)
        v_off = k_off + N_KV_HEADS * HEAD_DIM
        for hk in range(N_KV_HEADS):
            v = qkv_t[v_off + hk * HEAD_DIM:v_off + (hk + 1) * HEAD_DIM].astype(BF16)
            vt_ref[hk, s] = jnp.concatenate([v, ones_rows], axis=0)

    cur = project(0)
    for s in range(n_sub):
        nxt = project(s + 1) if s + 1 < n_sub else None
        emit(s, cur)
        cur = nxt


def _attn_qkv(h, g, w_t, q_gain, k_gain, cos_t, sin_t):
    tp = h.shape[0]
    nt = tp // TILE
    tm = _row_block(tp)
    n_sub = tm // TILE
    return pl.pallas_call(
        _qkv_body,
        grid=(tp // tm,),
        in_specs=[
            pl.BlockSpec((tm, D_MODEL), lambda i: (i, 0)),
            _const_spec((1, D_MODEL)),
            _const_spec((QKV_WIDTH, D_MODEL)),
            _const_spec((HEAD_DIM, 1)),
            _const_spec((HEAD_DIM, 1)),
            pl.BlockSpec((ROPE_AXIS, tm), lambda i: (0, i)),
            pl.BlockSpec((ROPE_AXIS, tm), lambda i: (0, i)),
        ],
        out_specs=[
            pl.BlockSpec((N_Q_HEADS * HEAD_DIM, tm), lambda i: (0, i)),
            pl.BlockSpec((n_sub, TILE, N_KV_HEADS * HEAD_DIM), lambda i: (i, 0, 0)),
            pl.BlockSpec((N_KV_HEADS, n_sub, V_ROWS, TILE), lambda i: (0, i, 0, 0)),
        ],
        out_shape=[
            jax.ShapeDtypeStruct((N_Q_HEADS * HEAD_DIM, tp), BF16),
            jax.ShapeDtypeStruct((nt, TILE, N_KV_HEADS * HEAD_DIM), BF16),
            jax.ShapeDtypeStruct((N_KV_HEADS, nt, V_ROWS, TILE), BF16),
        ],
        compiler_params=_cparams(("arbitrary",)),
        name="attn_qkv",
    )(h, g.reshape(1, -1), w_t, q_gain.reshape(-1, 1), k_gain.reshape(-1, 1), cos_t, sin_t)


def _attn_body(qt_ref, k_ref, vt_ref, o_ref, qaug_ref, vc_ref, sa_ref, sb_ref, *, n_chunks, n_q, meta_rows):
    kvh = pl.program_id(1)
    tq = TILE
    tpc = ATT_TK // TILE
    meta_tile = n_chunks * tpc
    n_steps = n_q * n_chunks

    qaug_ref[...] = jnp.zeros(qaug_ref.shape, BF16)
    row0 = pl.multiple_of(kvh * HEAD_DIM, HEAD_DIM)
    for j in range(n_q):
        for g in range(GROUP):
            qaug_ref[j * GROUP + g, pl.ds(row0, HEAD_DIM), :] = (
                qt_ref[g * HEAD_DIM:(g + 1) * HEAD_DIM, j * tq:(j + 1) * tq])

    def scores(t, g, s_ref):
        qi = t // n_chunks
        c = t % n_chunks
        kc = k_ref[pl.ds(c * tpc, tpc)].reshape(ATT_TK, N_KV_HEADS * HEAD_DIM)
        s = jnp.dot(kc, qaug_ref[qi * GROUP + g], preferred_element_type=F32)
        s_ref[g] = s
        return jnp.max(s, axis=0, keepdims=True)

    @pl.when(pl.program_id(2) == 0)
    def _():
        for c in range(n_chunks):
            for u in range(tpc):
                vc_ref[c, :, u * TILE:(u + 1) * TILE] = vt_ref[0, c * tpc + u]

    def v_chunk(c):
        return vc_ref[c]

    def softmax_pv(vc, g, s_ref, cmax, m, acc):
        m_new = jnp.maximum(m, cmax)
        alpha = jnp.exp2(m - m_new)
        p = jnp.exp2(s_ref[g] - m_new).astype(BF16)
        acc = alpha * acc + jnp.dot(vc, p, preferred_element_type=F32)
        return m_new, acc

    def finalize(qi, state):
        k_meta = k_ref[meta_tile, 0:META_KEY_ROWS, :]
        v_meta = vt_ref[0, meta_tile, :, 0:META_KEY_ROWS]
        for g in range(GROUP):
            m, acc = state[g]
            s = jnp.dot(k_meta, qaug_ref[qi * GROUP + g], preferred_element_type=F32)
            rows = lax.broadcasted_iota(jnp.int32, s.shape, 0)
            s = jnp.where(rows < meta_rows, s, NEG_BIG)
            m_new = jnp.maximum(m, jnp.max(s, axis=0, keepdims=True))
            p = jnp.exp2(s - m_new).astype(BF16)
            acc = jnp.exp2(m - m_new) * acc + jnp.dot(v_meta, p, preferred_element_type=F32)
            out_t = acc[:HEAD_DIM] / acc[HEAD_DIM:HEAD_DIM + 1]
            o_ref[pl.ds(pl.multiple_of(qi * tq, tq), tq), g * HEAD_DIM:(g + 1) * HEAD_DIM] = (
                out_t.T.astype(BF16))

    def step(t_next, next_ref, cur_ref, cmaxes, state, reset):
        c_cur = (t_next - 1) % n_chunks
        vc = v_chunk(c_cur)
        new_cmax, new_state = [], []
        for g in range(GROUP):
            m, acc = state[g]
            if reset:
                first = c_cur == 0
                m = jnp.where(first, NEG_BIG, m)
                acc = jnp.where(first, 0.0, acc)
            new_cmax.append(scores(t_next, g, next_ref))
            new_state.append(softmax_pv(vc, g, cur_ref, cmaxes[g], m, acc))
        return tuple(new_cmax), tuple(new_state)

    bufs = (sa_ref, sb_ref)
    unroll = ATT_UNROLL
    assert unroll % 2 == 0 and n_chunks % unroll == 0

    def group(i, carry):
        cmaxes, state = carry
        t0 = unroll * i
        for u in range(1, unroll + 1):
            cmaxes, state = step(t0 + u, bufs[u % 2], bufs[(u - 1) % 2], cmaxes, state, reset=(u == 1))

        @pl.when((t0 + unroll - 1) % n_chunks == n_chunks - 1)
        def _():
            finalize((t0 + unroll - 1) // n_chunks, state)

        return cmaxes, state

    state = tuple((jnp.full((1, tq), NEG_BIG, F32), jnp.zeros((V_ROWS, tq), F32)) for _ in range(GROUP))
    cmaxes = tuple(scores(0, g, sa_ref) for g in range(GROUP))
    n_groups = n_steps // unroll - 1
    cmaxes, state = lax.fori_loop(0, n_groups, group, (cmaxes, state))
    for t in range(unroll * n_groups + 1, n_steps):
        cmaxes, state = step(t, bufs[t % 2], bufs[(t - 1) % 2], cmaxes, state, reset=True)
    vc_last = v_chunk(n_chunks - 1)
    state = tuple(softmax_pv(vc_last, g, bufs[(n_steps - 1) % 2], cmaxes[g], *state[g])
                  for g in range(GROUP))
    finalize(n_q - 1, state)


def _attention(q_t, k, v_t, batch, n_tok, lp):
    tp = q_t.shape[1]
    n_chunks = n_tok // ATT_TK
    tiles_per_batch = lp // TILE
    n_q = max(d for d in range(1, ATT_NQ + 1) if tiles_per_batch % d == 0)
    q_steps = tiles_per_batch // n_q
    body = functools.partial(_attn_body, n_chunks=n_chunks, n_q=n_q, meta_rows=N_META)
    return pl.pallas_call(
        body,
        grid=(batch, N_KV_HEADS, q_steps),
        in_specs=[
            pl.BlockSpec((GROUP * HEAD_DIM, n_q * TILE), lambda b, h, i: (h, b * q_steps + i)),
            pl.BlockSpec((tiles_per_batch, TILE, N_KV_HEADS * HEAD_DIM), lambda b, h, i: (b, 0, 0)),
            pl.BlockSpec((1, tiles_per_batch, V_ROWS, TILE), lambda b, h, i: (h, b, 0, 0)),
        ],
        out_specs=pl.BlockSpec((n_q * TILE, GROUP * HEAD_DIM), lambda b, h, i: (b * q_steps + i, h)),
        out_shape=jax.ShapeDtypeStruct((tp, N_Q_HEADS * HEAD_DIM), BF16),
        scratch_shapes=[pltpu.VMEM((n_q * GROUP, N_KV_HEADS * HEAD_DIM, TILE), BF16),
                        pltpu.VMEM((n_chunks, V_ROWS, ATT_TK), BF16),
                        pltpu.VMEM((GROUP, ATT_TK, TILE), F32),
                        pltpu.VMEM((GROUP, ATT_TK, TILE), F32)],
        compiler_params=_cparams(("arbitrary", "arbitrary", "arbitrary")),
        name="attention",
    )(q_t, k, v_t)


def _rec_in_body(h_ref, g_ref, w_ref, gate_ref, xr_ref):
    hn = _rms_rows(h_ref[...], g_ref[...]).astype(BF16)
    u = jnp.dot(hn, w_ref[...], preferred_element_type=F32)
    x = u[:, :D_RNN]
    cdf = 0.5 * (1.0 + jnp.tanh(0.7978845608028654 * (x + 0.044715 * (x * x * x))))
    gate_ref[...] = (x * cdf).astype(gate_ref.dtype)
    xr_ref[...] = u[:, D_RNN:]


def _rec_in(h, g, w):
    tp = h.shape[0]
    tm = _row_block(tp)
    spec = pl.BlockSpec((tm, D_MODEL), lambda i: (i, 0))
    return pl.pallas_call(
        _rec_in_body,
        grid=(tp // tm,),
        in_specs=[spec, _const_spec((1, D_MODEL)), _const_spec((D_MODEL, 2 * D_RNN))],
        out_specs=[spec, spec],
        out_shape=[jax.ShapeDtypeStruct((tp, D_RNN), BF16), jax.ShapeDtypeStruct((tp, D_RNN), F32)],
        compiler_params=_cparams(("arbitrary",)),
        name="rec_in",
    )(h, g.reshape(1, -1), w)


def _rglru_coeffs(x, prev8, next8, cw_ref, cb_ref, wr_ref, wi_ref, br_ref, bi_ref, lam_ref):
    tt = x.shape[0]
    ext = jnp.concatenate([prev8, x, next8], axis=0)
    n_ext = tt + 16
    xm1 = pltpu.roll(ext, 1, axis=0)[8:8 + tt]
    xm2 = pltpu.roll(ext, 2, axis=0)[8:8 + tt]
    xp1 = pltpu.roll(ext, n_ext - 1, axis=0)[8:8 + tt]
    cw = cw_ref[...]
    xc = cb_ref[...] + xm2 * cw[0:1] + xm1 * cw[1:2] + x * cw[2:3] + xp1 * cw[3:4]
    k = (-RG_C * LOG2E) * jnp.logaddexp(-lam_ref[...], 0.0)
    r_parts = []
    i_parts = []
    for blk in range(N_RG_BLOCKS):
        xb = xc[:, blk * RG_BW:(blk + 1) * RG_BW].astype(BF16)
        r_parts.append(jnp.dot(xb, wr_ref[blk], preferred_element_type=F32))
        i_parts.append(jnp.dot(xb, wi_ref[blk], preferred_element_type=F32))
    r = jax.nn.sigmoid(jnp.concatenate(r_parts, axis=1) + br_ref[...])
    i = jax.nn.sigmoid(jnp.concatenate(i_parts, axis=1) + bi_ref[...])
    a = jnp.exp2(r * k)
    b = jnp.sqrt(1.0 - a * a) * i * xc
    return a, b


def _store_coeffs(x_ref, prev_ref, next_ref, is_tail, has_prev, has_next, a_s, b_s, gate_refs):
    @pl.when(jnp.logical_not(is_tail))
    def _():
        prev8 = jnp.where(has_prev, prev_ref[...], 0.0)
        next8 = jnp.where(has_next, next_ref[...], 0.0)
        a, b = _rglru_coeffs(x_ref[...], prev8, next8, *gate_refs)
        a_s[...] = a
        b_s[...] = b

    @pl.when(is_tail)
    def _():
        x = x_ref[...]
        zeros8 = jnp.zeros((8, x.shape[1]), F32)
        x = jnp.concatenate([x[:N_META], next_ref[...], x[N_META + 8:]], axis=0)
        a, b = _rglru_coeffs(x, zeros8, zeros8, *gate_refs)
        valid = lax.broadcasted_iota(jnp.int32, x.shape, 0) < N_META
        a_s[...] = jnp.where(valid, a, 1.0)
        b_s[...] = jnp.where(valid, b, 0.0)


def _scan_fwd_body(x_ref, prev_ref, next_ref, cw_ref, cb_ref, wr_ref, wi_ref, br_ref, bi_ref,
                   lam_ref, hf_ref, a_s, b_s, carry_s, *, n_tiles):
    j = pl.program_id(1)
    _store_coeffs(x_ref, prev_ref, next_ref, is_tail=j == 0, has_prev=j > 0, has_next=j < n_tiles - 1,
                  a_s=a_s, b_s=b_s,
                  gate_refs=(cw_ref, cb_ref, wr_ref, wi_ref, br_ref, bi_ref, lam_ref))

    @pl.when(j == 0)
    def _():
        carry_s[...] = jnp.zeros(carry_s.shape, F32)

    def step(r, c):
        c = a_s[pl.ds(r, 1), :] * c + b_s[pl.ds(r, 1), :]
        hf_ref[pl.ds(r, 1), :] = c
        return c

    carry_s[...] = lax.fori_loop(0, TILE, step, carry_s[...], unroll=8)


def _scan_bwd_body(x_ref, prev_ref, next_ref, cw_ref, cb_ref, wr_ref, wi_ref, br_ref, bi_ref,
                   lam_ref, hf_ref, gate_ref, h_ref, wout_ref, g_ref, o_ref,
                   a_s, b_s, hb_s, carry_s, *, n_tiles):
    j = pl.program_id(1)
    _store_coeffs(x_ref, prev_ref, next_ref, is_tail=j == n_tiles - 1, has_prev=j < n_tiles - 1,
                  has_next=j > 0, a_s=a_s, b_s=b_s,
                  gate_refs=(cw_ref, cb_ref, wr_ref, wi_ref, br_ref, bi_ref, lam_ref))

    @pl.when(j == 0)
    def _():
        carry_s[...] = jnp.zeros(carry_s.shape, F32)

    def step(t, c):
        r = TILE - 1 - t
        c = a_s[pl.ds(r, 1), :] * c + b_s[pl.ds(r, 1), :]
        hb_s[pl.ds(r, 1), :] = c
        return c

    carry_s[...] = lax.fori_loop(0, TILE, step, carry_s[...], unroll=8)
    y = ((hf_ref[...] + hb_s[...]) * gate_ref[...].astype(F32)).astype(BF16)
    m = jnp.dot(y, wout_ref[...], preferred_element_type=F32)
    o_ref[...] = h_ref[...] + _rms_rows(m, g_ref[...])


def _scan_specs(batch, n_tiles, reverse):
    n_x = n_tiles - 1
    rpt = TILE // 8

    def logical(j):
        return (n_tiles - 1 - j) if reverse else j

    def phys_tile(b, j):
        lj = logical(j)
        return b * n_tiles + jnp.where(lj == 0, n_x, lj - 1)

    def prev_blk(b, j):
        lj = logical(j)
        base = b * n_tiles * rpt
        idx = jnp.where(lj <= 1, n_x * rpt + 1, (lj - 1) * rpt - 1)
        return base + idx

    def next_blk(b, j):
        lj = logical(j)
        base = b * n_tiles * rpt
        idx = jnp.where(lj == 0, 0, jnp.minimum(lj, n_x - 1) * rpt)
        return base + idx

    tile_spec = pl.BlockSpec((TILE, D_RNN), lambda b, j: (phys_tile(b, j), 0))
    prev_spec = pl.BlockSpec((8, D_RNN), lambda b, j: (prev_blk(b, j), 0))
    next_spec = pl.BlockSpec((8, D_RNN), lambda b, j: (next_blk(b, j), 0))
    return tile_spec, prev_spec, next_spec


def _gate_specs():
    return [
        _const_spec((4, D_RNN)),
        _const_spec((1, D_RNN)),
        _const_spec((N_RG_BLOCKS, RG_BW, RG_BW)),
        _const_spec((N_RG_BLOCKS, RG_BW, RG_BW)),
        _const_spec((1, D_RNN)),
        _const_spec((1, D_RNN)),
        _const_spec((1, D_RNN)),
    ]


def _scan_fwd(xr, batch, conv_w, conv_b, wr, wi, br, bi, lam):
    tp = xr.shape[0]
    n_tiles = tp // TILE // batch
    tile_spec, prev_spec, next_spec = _scan_specs(batch, n_tiles, reverse=False)
    return pl.pallas_call(
        functools.partial(_scan_fwd_body, n_tiles=n_tiles),
        grid=(batch, n_tiles),
        in_specs=[tile_spec, prev_spec, next_spec] + _gate_specs(),
        out_specs=tile_spec,
        out_shape=jax.ShapeDtypeStruct((tp, D_RNN), F32),
        scratch_shapes=[pltpu.VMEM((TILE, D_RNN), F32), pltpu.VMEM((TILE, D_RNN), F32),
                        pltpu.VMEM((1, D_RNN), F32)],
        compiler_params=_cparams(("arbitrary", "arbitrary")),
        name="scan_fwd",
    )(xr, xr, xr, conv_w, conv_b, wr, wi, br, bi, lam)


def _scan_bwd(xr, batch, conv_w, conv_b, wr, wi, br, bi, lam, hf, gate, h, w_out, g):
    tp = xr.shape[0]
    n_tiles = tp // TILE // batch
    tile_spec, prev_spec, next_spec = _scan_specs(batch, n_tiles, reverse=True)
    return pl.pallas_call(
        functools.partial(_scan_bwd_body, n_tiles=n_tiles),
        grid=(batch, n_tiles),
        in_specs=[tile_spec, prev_spec, next_spec] + _gate_specs()
        + [tile_spec, tile_spec, tile_spec, _const_spec((D_RNN, D_MODEL)), _const_spec((1, D_MODEL))],
        out_specs=tile_spec,
        out_shape=jax.ShapeDtypeStruct((tp, D_MODEL), F32),
        scratch_shapes=[pltpu.VMEM((TILE, D_RNN), F32), pltpu.VMEM((TILE, D_RNN), F32),
                        pltpu.VMEM((TILE, D_RNN), F32), pltpu.VMEM((1, D_RNN), F32)],
        compiler_params=_cparams(("arbitrary", "arbitrary")),
        name="scan_bwd",
    )(xr, xr, xr, conv_w, conv_b, wr, wi, br, bi, lam, hf, gate, h, w_out, g.reshape(1, -1))


def _rope_tables_t(n_tok, lp, batch):
    n = jnp.arange(n_tok, dtype=jnp.int32)
    row = (n // GRID_W).astype(F32)
    col = (n % GRID_W).astype(F32)
    zeros = jnp.zeros((lp - n_tok,), F32)
    row = jnp.concatenate([row, zeros])
    col = jnp.concatenate([col, zeros])
    inv = 1.0 / (ROPE_THETA ** (jnp.arange(0, ROPE_AXIS, 2, dtype=F32) / ROPE_AXIS))
    theta_t = jnp.concatenate([inv[:, None] * row[None, :], inv[:, None] * col[None, :]], axis=0)
    theta_t = jnp.tile(theta_t, (1, batch))
    return jnp.cos(theta_t), jnp.sin(theta_t)


def kernel(x, meta_tokens, norm_gains, ffn_w_in, ffn_w_out, attn_w_qkv, attn_q_gain, attn_k_gain,
           attn_w_o, rec_w_in, rec_conv_w, rec_conv_b, rec_gate_w, rec_gate_b, rec_lambda, rec_w_out):
    batch, n_tok, d = x.shape
    assert d == D_MODEL and n_tok % TILE == 0 and N_META % 8 == 0 and N_META + 8 <= TILE
    lp = n_tok + TILE
    assert TILE % FFN_SUB == 0 and n_tok % (ATT_UNROLL * ATT_TK) == 0 and N_META <= META_KEY_ROWS <= TILE
    depth = norm_gains.shape[0]

    meta = jnp.broadcast_to(meta_tokens.astype(x.dtype)[None], (batch, N_META, d))
    pad = jnp.zeros((batch, TILE - N_META, d), x.dtype)
    h = jnp.concatenate([x, meta, pad], axis=1).reshape(batch * lp, d)
    cos_t, sin_t = _rope_tables_t(n_tok, lp, batch)

    w_in = ffn_w_in.astype(BF16)
    w_out = ffn_w_out.astype(BF16)
    for layer in range(depth):
        g = norm_gains[layer]
        h = _ffn(h, g[0], g[1], w_in, w_out, layer, 0)
        j = layer // 2
        if layer % 2 == 0:
            q_t, k, v_t = _attn_qkv(h, g[2], attn_w_qkv[j].T.astype(BF16), attn_q_gain[j],
                                    attn_k_gain[j], cos_t, sin_t)
            o = _attention(q_t, k, v_t, batch, n_tok, lp)
            h = _proj_ffn(o, h, attn_w_o[j].astype(BF16), g[3], g[4], g[5], w_in, w_out, layer, 1)
        else:
            gate, xr = _rec_in(h, g[2], rec_w_in[j].astype(BF16))
            gw = rec_gate_w[j].astype(BF16)
            gb = rec_gate_b[j]
            lam = rec_lambda[j]
            cw = rec_conv_w[j]
            cb = rec_conv_b[j].reshape(1, -1)
            hf = _scan_fwd(xr, batch, cw, cb, gw[0, 0], gw[0, 1], gb[0, 0].reshape(1, -1),
                           gb[0, 1].reshape(1, -1), lam[0].reshape(1, -1))
            h = _scan_bwd(xr, batch, cw, cb, gw[1, 0], gw[1, 1], gb[1, 0].reshape(1, -1),
                          gb[1, 1].reshape(1, -1), lam[1].reshape(1, -1), hf, gate, h,
                          rec_w_out[j].astype(BF16), g[3])
            h = _ffn(h, g[4], g[5], w_in, w_out, layer, 1)
    return h.reshape(batch, lp, d)[:, :n_tok]
```

```python
import functools

import jax
import jax.numpy as jnp
from jax import lax
from jax.experimental import pallas as pl
from jax.experimental.pallas import tpu as pltpu

D_MODEL = 1024
N_META = 16
GRID_W = 64
HEAD_DIM = 64
N_Q_HEADS = 16
N_KV_HEADS = 4
GROUP = 4
QKV_WIDTH = 1536
ROPE_AXIS = 32
ROPE_THETA = 10000.0
D_RNN = 1024
N_RG_BLOCKS = 4
RG_BW = 256
RG_C = 8.0
D_FF = 2816
NORM_EPS = 1e-6

VMEM_LIMIT_BYTES = 56 * 1024 * 1024

TILE = 256
ATT_TK = 1024
ATT_NQ = 13
ATT_UNROLL = 8
LOG2E = 1.4426950408889634
FFN_TM = 1280
FFN_SUB = 256
META_KEY_ROWS = 128
V_ROWS = 80
NEG_BIG = -1e30

BF16 = jnp.bfloat16
F32 = jnp.float32


SCAN_VMEM_LIMIT_BYTES = 40 * 1024 * 1024


def _cparams(sem, vmem_limit_bytes=VMEM_LIMIT_BYTES):
    return pltpu.CompilerParams(dimension_semantics=sem, vmem_limit_bytes=vmem_limit_bytes)


def _rms_rows(x, g):
    return x * lax.rsqrt(jnp.mean(x * x, axis=-1, keepdims=True) + NORM_EPS) * g


def _row_block(tp):
    return max(m for m in range(FFN_SUB, FFN_TM + 1, FFN_SUB) if tp % m == 0)


def _const_spec(shape):
    zeros = (0,) * len(shape)
    return pl.BlockSpec(shape, lambda *_: zeros)


FF_CHUNKS = (768, 768, 768, 512)


def _ffn_rows(load_x, n_sub, sub, gpre, gpost, win_ref, wout_ref, o_ref):
    def prep(s):
        x = load_x(s)
        return x, _rms_rows(x, gpre).astype(BF16)

    def finish(s, x, acc):
        o_ref[s * sub:(s + 1) * sub, :] = x + 0.5 * _rms_rows(acc, gpost)

    cur = prep(0)
    prev = None
    for s in range(n_sub):
        x, xn = cur
        acc = None
        c0 = 0
        for ci, ck in enumerate(FF_CHUNKS):
            a = jnp.dot(xn, win_ref[:, c0:c0 + ck], preferred_element_type=F32)
            b = jnp.dot(xn, win_ref[:, D_FF + c0:D_FF + c0 + ck], preferred_element_type=F32)
            g = (a * jax.nn.sigmoid(a) * b).astype(BF16)
            d = jnp.dot(g, wout_ref[c0:c0 + ck, :], preferred_element_type=F32)
            acc = d if acc is None else acc + d
            c0 += ck
            if ci == 0:
                if s + 1 < n_sub:
                    cur = prep(s + 1)
                if prev is not None:
                    finish(*prev)
        prev = (s, x, acc)
    finish(*prev)


def _ffn_body(h_ref, gpre_ref, gpost_ref, win_ref, wout_ref, o_ref):
    def load_x(s):
        return h_ref[s * FFN_SUB:(s + 1) * FFN_SUB, :]

    _ffn_rows(load_x, h_ref.shape[0] // FFN_SUB, FFN_SUB, gpre_ref[...], gpost_ref[...], win_ref, wout_ref, o_ref)


def _proj_ffn_body(a_ref, h_ref, wp_ref, gp_ref, gpre_ref, gpost_ref, win_ref, wout_ref, o_ref):
    def load_x(s):
        rows = slice(s * FFN_SUB, (s + 1) * FFN_SUB)
        m = jnp.dot(a_ref[rows, :], wp_ref[...], preferred_element_type=F32)
        return h_ref[rows, :] + _rms_rows(m, gp_ref[...])

    _ffn_rows(load_x, h_ref.shape[0] // FFN_SUB, FFN_SUB, gpre_ref[...], gpost_ref[...], win_ref, wout_ref, o_ref)


def _ffn_weight_specs(layer, idx):
    return [
        pl.BlockSpec((None, None, D_MODEL, 2 * D_FF), lambda i: (layer, idx, 0, 0),
                     pipeline_mode=pl.Buffered(1)),
        pl.BlockSpec((None, None, D_FF, D_MODEL), lambda i: (layer, idx, 0, 0),
                     pipeline_mode=pl.Buffered(1)),
    ]


def _ffn(h, gpre, gpost, w_in, w_out, layer, idx):
    tp = h.shape[0]
    tm = _row_block(tp)
    row_spec = pl.BlockSpec((tm, D_MODEL), lambda i: (i, 0))
    return pl.pallas_call(
        _ffn_body,
        grid=(tp // tm,),
        in_specs=[row_spec, _const_spec((1, D_MODEL)), _const_spec((1, D_MODEL))]
        + _ffn_weight_specs(layer, idx),
        out_specs=row_spec,
        out_shape=jax.ShapeDtypeStruct(h.shape, F32),
        compiler_params=_cparams(("arbitrary",)),
        name="ffn",
    )(h, gpre.reshape(1, -1), gpost.reshape(1, -1), w_in, w_out)


def _proj_ffn(a, h, wp, gp, gpre, gpost, w_in, w_out, layer, idx):
    tp, kdim = a.shape
    tm = _row_block(tp)
    row_spec = pl.BlockSpec((tm, D_MODEL), lambda i: (i, 0))
    return pl.pallas_call(
        _proj_ffn_body,
        grid=(tp // tm,),
        in_specs=[pl.BlockSpec((tm, kdim), lambda i: (i, 0)), row_spec,
                  pl.BlockSpec((kdim, D_MODEL), lambda i: (0, 0), pipeline_mode=pl.Buffered(1)),
                  _const_spec((1, D_MODEL)), _const_spec((1, D_MODEL)), _const_spec((1, D_MODEL))]
        + _ffn_weight_specs(layer, idx),
        out_specs=row_spec,
        out_shape=jax.ShapeDtypeStruct(h.shape, F32),
        compiler_params=_cparams(("arbitrary",)),
        name="proj_ffn",
    )(a, h, wp, gp.reshape(1, -1), gpre.reshape(1, -1), gpost.reshape(1, -1), w_in, w_out)


def _head_norm_rope(blk, gain, cos, sin):
    y = blk * lax.rsqrt(jnp.mean(blk * blk, axis=0, keepdims=True) + NORM_EPS) * gain
    x1 = y[:ROPE_AXIS]
    x2 = y[ROPE_AXIS:]
    return jnp.concatenate([x1 * cos - x2 * sin, x1 * sin + x2 * cos], axis=0)


def _qkv_body(h_ref, g_ref, wt_ref, qg_ref, kg_ref, cos_ref, sin_ref, qt_ref, k_ref, vt_ref):
    n_sub = k_ref.shape[0]
    qg = qg_ref[...]
    kg = kg_ref[...]
    ones_rows = (lax.broadcasted_iota(jnp.int32, (V_ROWS - HEAD_DIM, TILE), 0) == 0).astype(BF16)

    def project(s):
        hn = _rms_rows(h_ref[s * TILE:(s + 1) * TILE, :], g_ref[...]).astype(BF16)
        return lax.dot_general(wt_ref[...], hn, (((1,), (1,)), ((), ())),
                               preferred_element_type=F32)

    def emit(s, qkv_t):
        cols = slice(s * TILE, (s + 1) * TILE)
        cos = cos_ref[:, cols]
        sin = sin_ref[:, cols]
        for hq in range(N_Q_HEADS):
            blk = qkv_t[hq * HEAD_DIM:(hq + 1) * HEAD_DIM]
            q = _head_norm_rope(blk, qg, cos, sin) * (HEAD_DIM ** -0.5 * LOG2E)
            qt_ref[hq * HEAD_DIM:(hq + 1) * HEAD_DIM, cols] = q.astype(BF16)
        k_off = N_Q_HEADS * HEAD_DIM
        k_heads = []
        for hk in range(N_KV_HEADS):
            blk = qkv_t[k_off + hk * HEAD_DIM:k_off + (hk + 1) * HEAD_DIM]
            k_heads.append(_head_norm_rope(blk, kg, cos, sin))
        k_t = jnp.concatenate(k_heads, axis=0)
        k_ref[s] = k_t.T.astype(BF16)
        v_off = k_off + N_KV_HEADS * HEAD_DIM
        for hk in range(N_KV_HEADS):
            v = qkv_t[v_off + hk * HEAD_DIM:v_off + (hk + 1) * HEAD_DIM].astype(BF16)
            vt_ref[hk, s] = jnp.concatenate([v, ones_rows], axis=0)

    cur = project(0)
    for s in range(n_sub):
        nxt = project(s + 1) if s + 1 < n_sub else None
        emit(s, cur)
        cur = nxt


def _attn_qkv(h, g, w_t, q_gain, k_gain, cos_t, sin_t):
    tp = h.shape[0]
    nt = tp // TILE
    tm = _row_block(tp)
    n_sub = tm // TILE
    return pl.pallas_call(
        _qkv_body,
        grid=(tp // tm,),
        in_specs=[
            pl.BlockSpec((tm, D_MODEL), lambda i: (i, 0)),
            _const_spec((1, D_MODEL)),
            _const_spec((QKV_WIDTH, D_MODEL)),
            _const_spec((HEAD_DIM, 1)),
            _const_spec((HEAD_DIM, 1)),
            pl.BlockSpec((ROPE_AXIS, tm), lambda i: (0, i)),
            pl.BlockSpec((ROPE_AXIS, tm), lambda i: (0, i)),
        ],
        out_specs=[
            pl.BlockSpec((N_Q_HEADS * HEAD_DIM, tm), lambda i: (0, i)),
            pl.BlockSpec((n_sub, TILE, N_KV_HEADS * HEAD_DIM), lambda i: (i, 0, 0)),
            pl.BlockSpec((N_KV_HEADS, n_sub, V_ROWS, TILE), lambda i: (0, i, 0, 0)),
        ],
        out_shape=[
            jax.ShapeDtypeStruct((N_Q_HEADS * HEAD_DIM, tp), BF16),
            jax.ShapeDtypeStruct((nt, TILE, N_KV_HEADS * HEAD_DIM), BF16),
            jax.ShapeDtypeStruct((N_KV_HEADS, nt, V_ROWS, TILE), BF16),
        ],
        compiler_params=_cparams(("arbitrary",)),
        name="attn_qkv",
    )(h, g.reshape(1, -1), w_t, q_gain.reshape(-1, 1), k_gain.reshape(-1, 1), cos_t, sin_t)


def _attn_body(qt_ref, k_ref, vt_ref, o_ref, qaug_ref, vc_ref, sa_ref, sb_ref, *, n_chunks, n_q, meta_rows):
    kvh = pl.program_id(1)
    tq = TILE
    tpc = ATT_TK // TILE
    meta_tile = n_chunks * tpc
    n_steps = n_q * n_chunks

    qaug_ref[...] = jnp.zeros(qaug_ref.shape, BF16)
    row0 = pl.multiple_of(kvh * HEAD_DIM, HEAD_DIM)
    for j in range(n_q):
        for g in range(GROUP):
            qaug_ref[j * GROUP + g, pl.ds(row0, HEAD_DIM), :] = (
                qt_ref[g * HEAD_DIM:(g + 1) * HEAD_DIM, j * tq:(j + 1) * tq])

    def scores(t, g, s_ref):
        qi = t // n_chunks
        c = t % n_chunks
        kc = k_ref[pl.ds(c * tpc, tpc)].reshape(ATT_TK, N_KV_HEADS * HEAD_DIM)
        s = jnp.dot(kc, qaug_ref[qi * GROUP + g], preferred_element_type=F32)
        s_ref[g] = s
        return jnp.max(s, axis=0, keepdims=True)

    @pl.when(pl.program_id(2) == 0)
    def _():
        for c in range(n_chunks):
            for u in range(tpc):
                vc_ref[c, :, u * TILE:(u + 1) * TILE] = vt_ref[0, c * tpc + u]

    def v_chunk(c):
        return vc_ref[c]

    def softmax_pv(vc, g, s_ref, cmax, m, acc):
        m_new = jnp.maximum(m, cmax)
        alpha = jnp.exp2(m - m_new)
        p = jnp.exp2(s_ref[g] - m_new).astype(BF16)
        acc = alpha * acc + jnp.dot(vc, p, preferred_element_type=F32)
        return m_new, acc

    def finalize(qi, state):
        k_meta = k_ref[meta_tile, 0:META_KEY_ROWS, :]
        v_meta = vt_ref[0, meta_tile, :, 0:META_KEY_ROWS]
        for g in range(GROUP):
            m, acc = state[g]
            s = jnp.dot(k_meta, qaug_ref[qi * GROUP + g], preferred_element_type=F32)
            rows = lax.broadcasted_iota(jnp.int32, s.shape, 0)
            s = jnp.where(rows < meta_rows, s, NEG_BIG)
            m_new = jnp.maximum(m, jnp.max(s, axis=0, keepdims=True))
            p = jnp.exp2(s - m_new).astype(BF16)
            acc = jnp.exp2(m - m_new) * acc + jnp.dot(v_meta, p, preferred_element_type=F32)
            out_t = acc[:HEAD_DIM] / acc[HEAD_DIM:HEAD_DIM + 1]
            o_ref[pl.ds(pl.multiple_of(qi * tq, tq), tq), g * HEAD_DIM:(g + 1) * HEAD_DIM] = (
                out_t.T.astype(BF16))

    def step(t_next, next_ref, cur_ref, cmaxes, state, reset):
        c_cur = (t_next - 1) % n_chunks
        vc = v_chunk(c_cur)
        new_cmax, new_state = [], []
        for g in range(GROUP):
            m, acc = state[g]
            if reset:
                first = c_cur == 0
                m = jnp.where(first, NEG_BIG, m)
                acc = jnp.where(first, 0.0, acc)
            new_cmax.append(scores(t_next, g, next_ref))
            new_state.append(softmax_pv(vc, g, cur_ref, cmaxes[g], m, acc))
        return tuple(new_cmax), tuple(new_state)

    bufs = (sa_ref, sb_ref)
    unroll = ATT_UNROLL
    assert unroll % 2 == 0 and n_chunks % unroll == 0

    def group(i, carry):
        cmaxes, state = carry
        t0 = unroll * i
        for u in range(1, unroll + 1):
            cmaxes, state = step(t0 + u, bufs[u % 2], bufs[(u - 1) % 2], cmaxes, state, reset=(u == 1))

        @pl.when((t0 + unroll - 1) % n_chunks == n_chunks - 1)
        def _():
            finalize((t0 + unroll - 1) // n_chunks, state)

        return cmaxes, state

    state = tuple((jnp.full((1, tq), NEG_BIG, F32), jnp.zeros((V_ROWS, tq), F32)) for _ in range(GROUP))
    cmaxes = tuple(scores(0, g, sa_ref) for g in range(GROUP))
    n_groups = n_steps // unroll - 1
    cmaxes, state = lax.fori_loop(0, n_groups, group, (cmaxes, state))
    for t in range(unroll * n_groups + 1, n_steps):
        cmaxes, state = step(t, bufs[t % 2], bufs[(t - 1) % 2], cmaxes, state, reset=True)
    vc_last = v_chunk(n_chunks - 1)
    state = tuple(softmax_pv(vc_last, g, bufs[(n_steps - 1) % 2], cmaxes[g], *state[g])
                  for g in range(GROUP))
    finalize(n_q - 1, state)


def _attention(q_t, k, v_t, batch, n_tok, lp):
    tp = q_t.shape[1]
    n_chunks = n_tok // ATT_TK
    tiles_per_batch = lp // TILE
    n_q = max(d for d in range(1, ATT_NQ + 1) if tiles_per_batch % d == 0)
    q_steps = tiles_per_batch // n_q
    body = functools.partial(_attn_body, n_chunks=n_chunks, n_q=n_q, meta_rows=N_META)
    return pl.pallas_call(
        body,
        grid=(batch, N_KV_HEADS, q_steps),
        in_specs=[
            pl.BlockSpec((GROUP * HEAD_DIM, n_q * TILE), lambda b, h, i: (h, b * q_steps + i)),
            pl.BlockSpec((tiles_per_batch, TILE, N_KV_HEADS * HEAD_DIM), lambda b, h, i: (b, 0, 0)),
            pl.BlockSpec((1, tiles_per_batch, V_ROWS, TILE), lambda b, h, i: (h, b, 0, 0)),
        ],
        out_specs=pl.BlockSpec((n_q * TILE, GROUP * HEAD_DIM), lambda b, h, i: (b * q_steps + i, h)),
        out_shape=jax.ShapeDtypeStruct((tp, N_Q_HEADS * HEAD_DIM), BF16),
        scratch_shapes=[pltpu.VMEM((n_q * GROUP, N_KV_HEADS * HEAD_DIM, TILE), BF16),
                        pltpu.VMEM((n_chunks, V_ROWS, ATT_TK), BF16),
                        pltpu.VMEM((GROUP, ATT_TK, TILE), F32),
                        pltpu.VMEM((GROUP, ATT_TK, TILE), F32)],
        compiler_params=_cparams(("arbitrary", "arbitrary", "arbitrary")),
        name="attention",
    )(q_t, k, v_t)


def _rec_in_body(h_ref, g_ref, w_ref, gate_ref, xr_ref):
    hn = _rms_rows(h_ref[...], g_ref[...]).astype(BF16)
    u = jnp.dot(hn, w_ref[...], preferred_element_type=F32)
    x = u[:, :D_RNN]
    cdf = 0.5 * (1.0 + jnp.tanh(0.7978845608028654 * (x + 0.044715 * (x * x * x))))
    gate_ref[...] = x * cdf
    xr_ref[...] = u[:, D_RNN:]


def _rec_in(h, g, w):
    tp = h.shape[0]
    tm = _row_block(tp)
    spec = pl.BlockSpec((tm, D_MODEL), lambda i: (i, 0))
    return pl.pallas_call(
        _rec_in_body,
        grid=(tp // tm,),
        in_specs=[spec, _const_spec((1, D_MODEL)), _const_spec((D_MODEL, 2 * D_RNN))],
        out_specs=[spec, spec],
        out_shape=[jax.ShapeDtypeStruct((tp, D_RNN), F32)] * 2,
        compiler_params=_cparams(("arbitrary",)),
        name="rec_in",
    )(h, g.reshape(1, -1), w)


def _rglru_coeffs(x, prev8, next8, cw_ref, cb_ref, wr_ref, wi_ref, br_ref, bi_ref, lam_ref):
    tt = x.shape[0]
    ext = jnp.concatenate([prev8, x, next8], axis=0)
    n_ext = tt + 16
    xm1 = pltpu.roll(ext, 1, axis=0)[8:8 + tt]
    xm2 = pltpu.roll(ext, 2, axis=0)[8:8 + tt]
    xp1 = pltpu.roll(ext, n_ext - 1, axis=0)[8:8 + tt]
    cw = cw_ref[...]
    xc = cb_ref[...] + xm2 * cw[0:1] + xm1 * cw[1:2] + x * cw[2:3] + xp1 * cw[3:4]
    k = (-RG_C * LOG2E) * jnp.logaddexp(-lam_ref[...], 0.0)
    r_parts = []
    i_parts = []
    for blk in range(N_RG_BLOCKS):
        xb = xc[:, blk * RG_BW:(blk + 1) * RG_BW].astype(BF16)
        r_parts.append(jnp.dot(xb, wr_ref[blk], preferred_element_type=F32))
        i_parts.append(jnp.dot(xb, wi_ref[blk], preferred_element_type=F32))
    r = jax.nn.sigmoid(jnp.concatenate(r_parts, axis=1) + br_ref[...])
    i = jax.nn.sigmoid(jnp.concatenate(i_parts, axis=1) + bi_ref[...])
    a = jnp.exp2(r * k)
    b = jnp.sqrt(1.0 - a * a) * i * xc
    return a, b


def _store_coeffs(x_ref, prev_ref, next_ref, is_tail, has_prev, has_next, a_s, b_s, gate_refs):
    @pl.when(jnp.logical_not(is_tail))
    def _():
        prev8 = jnp.where(has_prev, prev_ref[...], 0.0)
        next8 = jnp.where(has_next, next_ref[...], 0.0)
        a, b = _rglru_coeffs(x_ref[...], prev8, next8, *gate_refs)
        a_s[...] = a
        b_s[...] = b

    @pl.when(is_tail)
    def _():
        x = x_ref[...]
        zeros8 = jnp.zeros((8, x.shape[1]), F32)
        x = jnp.concatenate([x[:N_META], next_ref[...], x[N_META + 8:]], axis=0)
        a, b = _rglru_coeffs(x, zeros8, zeros8, *gate_refs)
        valid = lax.broadcasted_iota(jnp.int32, x.shape, 0) < N_META
        a_s[...] = jnp.where(valid, a, 1.0)
        b_s[...] = jnp.where(valid, b, 0.0)


def _scan_fwd_body(x_ref, prev_ref, next_ref, cw_ref, cb_ref, wr_ref, wi_ref, br_ref, bi_ref,
                   lam_ref, hf_ref, a_s, b_s, carry_s, *, n_tiles):
    j = pl.program_id(1)
    _store_coeffs(x_ref, prev_ref, next_ref, is_tail=j == 0, has_prev=j > 0, has_next=j < n_tiles - 1,
                  a_s=a_s, b_s=b_s,
                  gate_refs=(cw_ref, cb_ref, wr_ref, wi_ref, br_ref, bi_ref, lam_ref))

    @pl.when(j == 0)
    def _():
        carry_s[...] = jnp.zeros(carry_s.shape, F32)

    def step(r, c):
        c = a_s[pl.ds(r, 1), :] * c + b_s[pl.ds(r, 1), :]
        hf_ref[pl.ds(r, 1), :] = c
        return c

    carry_s[...] = lax.fori_loop(0, TILE, step, carry_s[...], unroll=8)


def _scan_bwd_body(x_ref, prev_ref, next_ref, cw_ref, cb_ref, wr_ref, wi_ref, br_ref, bi_ref,
                   lam_ref, hf_ref, gate_ref, h_ref, wout_ref, g_ref, o_ref,
                   a_s, b_s, hb_s, carry_s, *, n_tiles):
    j = pl.program_id(1)
    _store_coeffs(x_ref, prev_ref, next_ref, is_tail=j == n_tiles - 1, has_prev=j < n_tiles - 1,
                  has_next=j > 0, a_s=a_s, b_s=b_s,
                  gate_refs=(cw_ref, cb_ref, wr_ref, wi_ref, br_ref, bi_ref, lam_ref))

    @pl.when(j == 0)
    def _():
        carry_s[...] = jnp.zeros(carry_s.shape, F32)

    def step(t, c):
        r = TILE - 1 - t
        c = a_s[pl.ds(r, 1), :] * c + b_s[pl.ds(r, 1), :]
        hb_s[pl.ds(r, 1), :] = c
        return c

    carry_s[...] = lax.fori_loop(0, TILE, step, carry_s[...], unroll=8)
    y = ((hf_ref[...] + hb_s[...]) * gate_ref[...]).astype(BF16)
    m = jnp.dot(y, wout_ref[...], preferred_element_type=F32)
    o_ref[...] = h_ref[...] + _rms_rows(m, g_ref[...])


def _scan_specs(batch, n_tiles, reverse):
    n_x = n_tiles - 1
    rpt = TILE // 8

    def logical(j):
        return (n_tiles - 1 - j) if reverse else j

    def phys_tile(b, j):
        lj = logical(j)
        return b * n_tiles + jnp.where(lj == 0, n_x, lj - 1)

    def prev_blk(b, j):
        lj = logical(j)
        base = b * n_tiles * rpt
        idx = jnp.where(lj <= 1, n_x * rpt + 1, (lj - 1) * rpt - 1)
        return base + idx

    def next_blk(b, j):
        lj = logical(j)
        base = b * n_tiles * rpt
        idx = jnp.where(lj == 0, 0, jnp.minimum(lj, n_x - 1) * rpt)
        return base + idx

    tile_spec = pl.BlockSpec((TILE, D_RNN), lambda b, j: (phys_tile(b, j), 0))
    prev_spec = pl.BlockSpec((8, D_RNN), lambda b, j: (prev_blk(b, j), 0))
    next_spec = pl.BlockSpec((8, D_RNN), lambda b, j: (next_blk(b, j), 0))
    return tile_spec, prev_spec, next_spec


def _gate_specs():
    return [
        _const_spec((4, D_RNN)),
        _const_spec((1, D_RNN)),
        _const_spec((N_RG_BLOCKS, RG_BW, RG_BW)),
        _const_spec((N_RG_BLOCKS, RG_BW, RG_BW)),
        _const_spec((1, D_RNN)),
        _const_spec((1, D_RNN)),
        _const_spec((1, D_RNN)),
    ]


def _scan_fwd(xr, batch, conv_w, conv_b, wr, wi, br, bi, lam):
    tp = xr.shape[0]
    n_tiles = tp // TILE // batch
    tile_spec, prev_spec, next_spec = _scan_specs(batch, n_tiles, reverse=False)
    return pl.pallas_call(
        functools.partial(_scan_fwd_body, n_tiles=n_tiles),
        grid=(batch, n_tiles),
        in_specs=[tile_spec, prev_spec, next_spec] + _gate_specs(),
        out_specs=tile_spec,
        out_shape=jax.ShapeDtypeStruct((tp, D_RNN), F32),
        scratch_shapes=[pltpu.VMEM((TILE, D_RNN), F32), pltpu.VMEM((TILE, D_RNN), F32),
                        pltpu.VMEM((1, D_RNN), F32)],
        compiler_params=_cparams(("arbitrary", "arbitrary"), SCAN_VMEM_LIMIT_BYTES),
        name="scan_fwd",
    )(xr, xr, xr, conv_w, conv_b, wr, wi, br, bi, lam)


def _scan_bwd(xr, batch, conv_w, conv_b, wr, wi, br, bi, lam, hf, gate, h, w_out, g):
    tp = xr.shape[0]
    n_tiles = tp // TILE // batch
    tile_spec, prev_spec, next_spec = _scan_specs(batch, n_tiles, reverse=True)
    return pl.pallas_call(
        functools.partial(_scan_bwd_body, n_tiles=n_tiles),
        grid=(batch, n_tiles),
        in_specs=[tile_spec, prev_spec, next_spec] + _gate_specs()
        + [tile_spec, tile_spec, tile_spec, _const_spec((D_RNN, D_MODEL)), _const_spec((1, D_MODEL))],
        out_specs=tile_spec,
        out_shape=jax.ShapeDtypeStruct((tp, D_MODEL), F32),
        scratch_shapes=[pltpu.VMEM((TILE, D_RNN), F32), pltpu.VMEM((TILE, D_RNN), F32),
                        pltpu.VMEM((TILE, D_RNN), F32), pltpu.VMEM((1, D_RNN), F32)],
        compiler_params=_cparams(("arbitrary", "arbitrary"), SCAN_VMEM_LIMIT_BYTES),
        name="scan_bwd",
    )(xr, xr, xr, conv_w, conv_b, wr, wi, br, bi, lam, hf, gate, h, w_out, g.reshape(1, -1))


def _rope_tables_t(n_tok, lp, batch):
    n = jnp.arange(n_tok, dtype=jnp.int32)
    row = (n // GRID_W).astype(F32)
    col = (n % GRID_W).astype(F32)
    zeros = jnp.zeros((lp - n_tok,), F32)
    row = jnp.concatenate([row, zeros])
    col = jnp.concatenate([col, zeros])
    inv = 1.0 / (ROPE_THETA ** (jnp.arange(0, ROPE_AXIS, 2, dtype=F32) / ROPE_AXIS))
    theta_t = jnp.concatenate([inv[:, None] * row[None, :], inv[:, None] * col[None, :]], axis=0)
    theta_t = jnp.tile(theta_t, (1, batch))
    return jnp.cos(theta_t), jnp.sin(theta_t)


def kernel(x, meta_tokens, norm_gains, ffn_w_in, ffn_w_out, attn_w_qkv, attn_q_gain, attn_k_gain,
           attn_w_o, rec_w_in, rec_conv_w, rec_conv_b, rec_gate_w, rec_gate_b, rec_lambda, rec_w_out):
    batch, n_tok, d = x.shape
    assert d == D_MODEL and n_tok % TILE == 0 and N_META % 8 == 0 and N_META + 8 <= TILE
    lp = n_tok + TILE
    assert TILE % FFN_SUB == 0 and n_tok % (ATT_UNROLL * ATT_TK) == 0 and N_META <= META_KEY_ROWS <= TILE
    depth = norm_gains.shape[0]

    meta = jnp.broadcast_to(meta_tokens.astype(x.dtype)[None], (batch, N_META, d))
    pad = jnp.zeros((batch, TILE - N_META, d), x.dtype)
    h = jnp.concatenate([x, meta, pad], axis=1).reshape(batch * lp, d)
    cos_t, sin_t = _rope_tables_t(n_tok, lp, batch)

    w_in = ffn_w_in.astype(BF16)
    w_out = ffn_w_out.astype(BF16)
    for layer in range(depth):
        g = norm_gains[layer]
        h = _ffn(h, g[0], g[1], w_in, w_out, layer, 0)
        j = layer // 2
        if layer % 2 == 0:
            q_t, k, v_t = _attn_qkv(h, g[2], attn_w_qkv[j].T.astype(BF16), attn_q_gain[j],
                                    attn_k_gain[j], cos_t, sin_t)
            o = _attention(q_t, k, v_t, batch, n_tok, lp)
            h = _proj_ffn(o, h, attn_w_o[j].astype(BF16), g[3], g[4], g[5], w_in, w_out, layer, 1)
        else:
            gate, xr = _rec_in(h, g[2], rec_w_in[j].astype(BF16))
            gw = rec_gate_w[j].astype(BF16)
            gb = rec_gate_b[j]
            lam = rec_lambda[j]
            cw = rec_conv_w[j]
            cb = rec_conv_b[j].reshape(1, -1)
            hf = _scan_fwd(xr, batch, cw, cb, gw[0, 0], gw[0, 1], gb[0, 0].reshape(1, -1),
                           gb[0, 1].reshape(1, -1), lam[0].reshape(1, -1))
            h = _scan_bwd(xr, batch, cw, cb, gw[1, 0], gw[1, 1], gb[1, 0].reshape(1, -1),
                          gb[1, 1].reshape(1, -1), lam[1].reshape(1, -1), hf, gate, h,
                          rec_w_out[j].astype(BF16), g[3])
            h = _ffn(h, g[4], g[5], w_in, w_out, layer, 1)
    return h.reshape(batch, lp, d)[:, :n_tok]
```
